```python
import jax, jax.numpy as jnp
from jax import lax
import numpy as np

D_MODEL = 1024
BATCH = 4
SEQ = 8192
DEPTH = 1

GLA_HEADS = 4
GLA_DK = 64
GLA_DV = 128
GLA_GATE_RANK = 16
GLA_GATE_TAU = 16.0
GLA_CHUNK = 64
GLA_QK = GLA_HEADS * GLA_DK
GLA_V = GLA_HEADS * GLA_DV
ATTN_HEADS = 8
ATTN_HEAD_DIM = 64
ATTN_DIM = ATTN_HEADS * ATTN_HEAD_DIM
DILATED_PAIRS = ((128, 1), (512, 4), (2048, 16))
ATTN_BLOCK = 128
MIX_WIDTH = GLA_V + ATTN_DIM
IN_SPLITS = (GLA_QK, GLA_QK, GLA_V, GLA_V, GLA_GATE_RANK, ATTN_DIM, ATTN_DIM, ATTN_DIM)
IN_WIDTH = sum(IN_SPLITS)
D_FF = 2816
CONV_WIDTH = 3
N_MOD = 6
EPS = 1e-6

kernel_name = "hybrid_gla_dilated_attn_convffn_block"


def rms_norm(x, g):
    xf = x.astype(jnp.float32)
    y = xf * lax.rsqrt(jnp.mean(xf * xf, axis=-1, keepdims=True) + EPS)
    return (y * g.astype(jnp.float32)).astype(x.dtype)


def alibi_slopes(n):
    return jnp.asarray([2.0 ** (-8.0 * (h + 1) / n) for h in range(n)], dtype=jnp.float32)


def gla_group(q, k, v, log_a, r, norm_g):
    B, S, H, _ = q.shape
    n = S // GLA_CHUNK

    def chunks(t):
        return t.reshape(B, n, GLA_CHUNK, H, t.shape[-1]).transpose(1, 0, 3, 2, 4).astype(jnp.float32)

    qc = chunks(q) * (GLA_DK ** -0.5)
    kc, vc, gc = chunks(k), chunks(v), chunks(log_a)
    causal = jnp.tril(jnp.ones((GLA_CHUNK, GLA_CHUNK), dtype=bool))

    def step(state, inp):
        qi, ki, vi, gi = inp
        b = jnp.cumsum(gi, axis=-2)
        b_last = b[..., -1, :]
        o_inter = jnp.einsum('bhck,bhkv->bhcv', qi * jnp.exp(b), state)
        diff = b[:, :, :, None, :] - b[:, :, None, :, :]
        decay = jnp.exp(jnp.where(causal[:, :, None], diff, -jnp.inf))
        scores = jnp.einsum('bhik,bhjk,bhijk->bhij', qi, ki, decay)
        o = o_inter + jnp.einsum('bhij,bhjv->bhiv', scores, vi)
        state = state * jnp.exp(b_last)[..., None] + jnp.einsum(
            'bhck,bhcv->bhkv', ki * jnp.exp(b_last[:, :, None, :] - b), vi)
        return state, o

    state0 = jnp.zeros((B, H, GLA_DK, GLA_DV), jnp.float32)
    _, o = lax.scan(step, state0, (qc, kc, vc, gc))
    o = o.transpose(1, 0, 3, 2, 4).reshape(B, S, H, GLA_DV)
    o = rms_norm(o, norm_g).reshape(B, S, H * GLA_DV)
    return (o * jax.nn.silu(r.astype(jnp.float32))).astype(r.dtype)


def dilated_branch(q, k, v, slopes, window, dilation):
    B, S, H, E = q.shape
    L = S // dilation
    span = window // dilation
    nb = -(-L // ATTN_BLOCK)
    pad = nb * ATTN_BLOCK - L

    def to_blocks(t):
        t = t.reshape(B, L, dilation, H, E).transpose(0, 2, 3, 1, 4)
        t = jnp.pad(t, ((0, 0), (0, 0), (0, 0), (0, pad), (0, 0)))
        return t.reshape(B, dilation, H, nb, ATTN_BLOCK, E)

    def with_prev(t):
        prev = jnp.pad(t, ((0, 0), (0, 0), (0, 0), (1, 0), (0, 0), (0, 0)))[:, :, :, :-1]
        return jnp.concatenate([prev, t], axis=4)

    qb = to_blocks(q)
    kw = with_prev(to_blocks(k))
    vw = with_prev(to_blocks(v))
    s = jnp.einsum('bdhnqe,bdhnke->bdhnqk', qb, kw).astype(jnp.float32) * (E ** -0.5)
    iq = jnp.arange(ATTN_BLOCK)[:, None]
    ik = jnp.arange(2 * ATTN_BLOCK)[None, :]
    rel = iq + ATTN_BLOCK - ik
    key_idx = jnp.arange(nb)[:, None, None] * ATTN_BLOCK + ik - ATTN_BLOCK
    valid = (rel >= 0) & (rel <= span) & (key_idx >= 0)
    alibi = -slopes[:, None, None, None] * (dilation * rel).astype(jnp.float32)
    s = jnp.where(valid, s + alibi, -jnp.inf)
    m = jnp.max(s, axis=-1, keepdims=True)
    p = jnp.exp(s - m)
    den = jnp.sum(p, axis=-1, keepdims=True)
    o = jnp.einsum('bdhnqk,bdhnke->bdhnqe', p, vw.astype(jnp.float32)) / den
    lse = (m + jnp.log(den))[..., 0]
    o = o.reshape(B, dilation, H, nb * ATTN_BLOCK, E)[:, :, :, :L]
    o = o.transpose(0, 3, 1, 2, 4).reshape(B, S, H, E)
    lse = lse.reshape(B, dilation, H, nb * ATTN_BLOCK)[:, :, :, :L]
    lse = lse.transpose(0, 3, 1, 2).reshape(B, S, H)
    return o, lse


def dilated_attention_group(q, k, v):
    slopes = alibi_slopes(ATTN_HEADS)
    outs, lses = [], []
    for window, dilation in DILATED_PAIRS:
        o, lse = dilated_branch(q, k, v, slopes, window, dilation)
        outs.append(o)
        lses.append(lse)
    weights = jax.nn.softmax(jnp.stack(lses, axis=0), axis=0)
    return jnp.einsum('gbsh,gbshe->bshe', weights, jnp.stack(outs, axis=0))


def causal_depthwise_conv(u, w, b):
    K, C = w.shape
    y = lax.conv_general_dilated(u, w[:, None, :], window_strides=(1,), padding=[(K - 1, 0)],
                                 dimension_numbers=('NWC', 'WIO', 'NWC'), feature_group_count=C)
    return y + b


def setup_inputs(seed: int = 0) -> dict:
    key = jax.random.key(seed)
    ks = jax.random.split(key, 17)
    f32 = jnp.float32
    L = DEPTH

    def nrm(k, shape, scale):
        return jax.random.normal(k, shape, f32) * scale

    return {
        "x": nrm(ks[0], (BATCH, SEQ, D_MODEL), 1.0),
        "c": nrm(ks[1], (BATCH, D_MODEL), 1.0),
        "w_ada": nrm(ks[2], (L, D_MODEL, N_MOD * D_MODEL), 0.5 * D_MODEL ** -0.5),
        "b_ada": nrm(ks[3], (L, N_MOD * D_MODEL), 0.02),
        "norm1_g": 1.0 + nrm(ks[4], (L, D_MODEL), 0.02),
        "w_in": nrm(ks[5], (L, D_MODEL, IN_WIDTH), D_MODEL ** -0.5),
        "gla_w_gate": nrm(ks[6], (L, GLA_GATE_RANK, GLA_QK), GLA_GATE_RANK ** -0.5),
        "gla_b_gate": nrm(ks[7], (L, GLA_QK), 0.02),
        "gla_norm_g": 1.0 + nrm(ks[8], (L, GLA_DV), 0.02),
        "q_norm_g": 1.0 + nrm(ks[9], (L, ATTN_HEAD_DIM), 0.02),
        "k_norm_g": 1.0 + nrm(ks[10], (L, ATTN_HEAD_DIM), 0.02),
        "w_out": nrm(ks[11], (L, MIX_WIDTH, D_MODEL), MIX_WIDTH ** -0.5),
        "norm2_g": 1.0 + nrm(ks[12], (L, D_MODEL), 0.02),
        "w_up": nrm(ks[13], (L, D_MODEL, 2 * D_FF), D_MODEL ** -0.5),
        "conv_w": nrm(ks[14], (L, CONV_WIDTH, 2 * D_FF), CONV_WIDTH ** -0.5),
        "conv_b": nrm(ks[15], (L, 2 * D_FF), 0.02),
        "w_down": nrm(ks[16], (L, D_FF, D_MODEL), D_FF ** -0.5),
    }


def reference(x, c, w_ada, b_ada, norm1_g, w_in, gla_w_gate, gla_b_gate, gla_norm_g,
              q_norm_g, k_norm_g, w_out, norm2_g, w_up, conv_w, conv_b, w_down):
    B, S, _ = x.shape
    cond = jax.nn.silu(c)
    split_at = np.cumsum(IN_SPLITS)[:-1].tolist()
    for l in range(DEPTH):
        mod = (cond @ w_ada[l] + b_ada[l])[:, None, :]
        sh1, sc1, g1, sh2, sc2, g2 = jnp.split(mod, N_MOD, axis=-1)

        h = rms_norm(x, norm1_g[l]) * (1 + sc1) + sh1
        proj = h @ w_in[l]
        gq, gk, gv, gr, glr, aq, ak, av = jnp.split(proj, split_at, axis=-1)
        log_a = jax.nn.log_sigmoid((glr @ gla_w_gate[l] + gla_b_gate[l]).astype(jnp.float32)) / GLA_GATE_TAU
        y_gla = gla_group(gq.reshape(B, S, GLA_HEADS, GLA_DK), gk.reshape(B, S, GLA_HEADS, GLA_DK),
                          gv.reshape(B, S, GLA_HEADS, GLA_DV), log_a.reshape(B, S, GLA_HEADS, GLA_DK),
                          gr, gla_norm_g[l])
        qa = rms_norm(aq.reshape(B, S, ATTN_HEADS, ATTN_HEAD_DIM), q_norm_g[l])
        ka = rms_norm(ak.reshape(B, S, ATTN_HEADS, ATTN_HEAD_DIM), k_norm_g[l])
        y_att = dilated_attention_group(qa, ka, av.reshape(B, S, ATTN_HEADS, ATTN_HEAD_DIM))
        mixed = jnp.concatenate([y_gla.astype(x.dtype), y_att.reshape(B, S, ATTN_DIM).astype(x.dtype)], axis=-1)
        x = x + g1 * (mixed @ w_out[l])

        h = rms_norm(x, norm2_g[l]) * (1 + sc2) + sh2
        u = causal_depthwise_conv(h @ w_up[l], conv_w[l], conv_b[l])
        u_gate, u_val = jnp.split(u, 2, axis=-1)
        x = x + g2 * ((jax.nn.silu(u_gate) * u_val) @ w_down[l])
    return x
```

```python
import functools

import jax
import jax.numpy as jnp
import numpy as np
from jax import lax
from jax.experimental import pallas as pl
from jax.experimental.pallas import tpu as pltpu

F32 = jnp.float32
BF16 = jnp.bfloat16

D_MODEL = 1024
GLA_HEADS = 4
GLA_DK = 64
GLA_DV = 128
GLA_GATE_RANK = 16
GLA_GATE_TAU = 16.0
GLA_CHUNK = 64
GLA_QK = GLA_HEADS * GLA_DK
GLA_V = GLA_HEADS * GLA_DV
ATTN_HEADS = 8
ATTN_HEAD_DIM = 64
ATTN_DIM = ATTN_HEADS * ATTN_HEAD_DIM
DILATED_PAIRS = ((128, 1), (512, 4), (2048, 16))
ATTN_BLOCK = 128
D_FF = 2816
CONV_WIDTH = 3
N_MOD = 6
EPS = 1e-6

LANES = 128
NEG_BIG = -1e30

INPROJ_TM = 512
GLA_T = 512
ATTN_TQ = 512
FFN_TM = 256
FFN_CH = 256
VMEM_LIMIT = 56 * 1024 * 1024


def _const_spec(shape):
    nd = len(shape)
    return pl.BlockSpec(shape, lambda *_: (0,) * nd, pipeline_mode=pl.Buffered(1))


def _dot(a, b):
    return jnp.dot(a, b, preferred_element_type=F32)


def _dot_nt(a, b):
    return lax.dot_general(a, b, (((1,), (1,)), ((), ())), preferred_element_type=F32)


def _dot_tn(a, b):
    return lax.dot_general(a, b, (((0,), (0,)), ((), ())), preferred_element_type=F32)


def _split_bf16(x):
    hi = x.astype(BF16)
    lo = (x - hi.astype(F32)).astype(BF16)
    return hi, lo


def _ada_kernel(c_ref, w_ref, b_ref, o_ref):
    c = c_ref[...]
    cond = c / (1.0 + jnp.exp(-c))
    o_ref[...] = _dot(cond.astype(BF16), w_ref[...].astype(BF16)) + b_ref[...]


def _ada(c, w_ada, b_ada):
    B = c.shape[0]
    n = w_ada.shape[1]
    bn = D_MODEL
    return pl.pallas_call(
        _ada_kernel,
        out_shape=jax.ShapeDtypeStruct((B, n), F32),
        grid=(n // bn,),
        in_specs=[
            pl.BlockSpec((B, D_MODEL), lambda j: (0, 0)),
            pl.BlockSpec((D_MODEL, bn), lambda j: (0, j)),
            pl.BlockSpec((1, bn), lambda j: (0, j)),
        ],
        out_specs=pl.BlockSpec((B, bn), lambda j: (0, j)),
        name="ada",
    )(c, w_ada, b_ada.reshape(1, n))


def _inproj_kernel(x_ref, mod_ref, g1_ref, w_ref, wglr_ref, wgate_ref, bgate_ref, ones_ref,
                   qg_ref, kg_ref,
                   gq_ref, gk_ref, la_ref, gv_ref, gr_ref, aq_ref, ak_ref, av_ref):
    x = x_ref[...]
    ms = jnp.mean(x * x, axis=-1, keepdims=True)
    y = x * lax.rsqrt(ms + EPS) * g1_ref[...]
    h = y * (1.0 + mod_ref[1:2, :]) + mod_ref[0:1, :]
    hb = h.astype(BF16)

    def proj(c0, n):
        return _dot(hb, w_ref[:, c0:c0 + n])

    gq_ref[...] = proj(0, GLA_QK) * (GLA_DK ** -0.5)
    gk_ref[...] = proj(GLA_QK, GLA_QK)
    gv_ref[...] = proj(2 * GLA_QK, GLA_V).astype(BF16)
    gr_ref[...] = proj(2 * GLA_QK + GLA_V, GLA_V)

    glr = _dot(hb, wglr_ref[...])
    z = _dot(glr.astype(BF16), wgate_ref[...]) + bgate_ref[...]
    log_sig = jnp.minimum(z, 0.0) - jnp.log1p(jnp.exp(-jnp.abs(z)))
    la_ref[...] = log_sig * (1.0 / GLA_GATE_TAU)

    def head_norm(p, gain):
        hi, lo = _split_bf16(p * p)
        ssum = _dot(hi, ones_ref[...]) + _dot(lo, ones_ref[...])
        return p * lax.rsqrt(ssum * (1.0 / ATTN_HEAD_DIM) + EPS) * gain

    a0 = 2 * GLA_QK + 2 * GLA_V
    aq = head_norm(proj(a0, ATTN_DIM), qg_ref[...])
    aq_ref[...] = (aq * (ATTN_HEAD_DIM ** -0.5)).astype(BF16)
    ak_ref[...] = head_norm(proj(a0 + ATTN_DIM, ATTN_DIM), kg_ref[...]).astype(BF16)
    av_ref[...] = proj(a0 + 2 * ATTN_DIM, ATTN_DIM).astype(BF16)


def _inproj(x, mod3, g1, w_main, w_glr, w_gate, b_gate, ones_blk, qg, kg):
    B, S, _ = x.shape
    tm = INPROJ_TM
    tok = lambda n: pl.BlockSpec((None, tm, n), lambda b, i: (b, i, 0))
    out_dims = [(GLA_QK, F32), (GLA_QK, F32), (GLA_QK, F32), (GLA_V, BF16), (GLA_V, F32),
                (ATTN_DIM, BF16), (ATTN_DIM, BF16), (ATTN_DIM, BF16)]
    return pl.pallas_call(
        _inproj_kernel,
        out_shape=[jax.ShapeDtypeStruct((B, S, n), dt) for n, dt in out_dims],
        grid=(B, S // tm),
        in_specs=[
            tok(D_MODEL),
            pl.BlockSpec((None, N_MOD, D_MODEL), lambda b, i: (b, 0, 0)),
            _const_spec(g1.shape), _const_spec(w_main.shape), _const_spec(w_glr.shape),
            _const_spec(w_gate.shape), _const_spec(b_gate.shape), _const_spec(ones_blk.shape),
            _const_spec(qg.shape), _const_spec(kg.shape),
        ],
        out_specs=[tok(n) for n, _ in out_dims],
        compiler_params=pltpu.CompilerParams(
            dimension_semantics=("parallel", "parallel"), vmem_limit_bytes=VMEM_LIMIT),
        name="inproj",
    )(x, mod3, g1, w_main, w_glr, w_gate, b_gate, ones_blk, qg, kg)


def _pair_blockdiag(a, lane_lo):
    zero = jnp.zeros_like(a)
    return jnp.concatenate([jnp.where(lane_lo, a, zero), jnp.where(lane_lo, zero, a)],
                           axis=0).astype(BF16)


def _gla_kernel(q_ref, k_ref, la_ref, v_ref, r_ref, g_ref, o_ref, st_ref):
    C = GLA_CHUNK

    @pl.when(pl.program_id(1) == 0)
    def _():
        st_ref[...] = jnp.zeros_like(st_ref)

    row = lax.broadcasted_iota(jnp.int32, (C, C), 0)
    col = lax.broadcasted_iota(jnp.int32, (C, C), 1)
    tri = jnp.where(col <= row, 1.0, 0.0).astype(BF16)
    row2 = lax.broadcasted_iota(jnp.int32, (2 * C, C), 0)
    col2 = lax.broadcasted_iota(jnp.int32, (2 * C, C), 1)
    causal = col2 <= jnp.where(row2 >= C, row2 - C, row2)
    lane_lo = lax.broadcasted_iota(jnp.int32, (C, LANES), 1) < GLA_DK
    lane_lo_v = lax.broadcasted_iota(jnp.int32, (GLA_DV, LANES), 1) < GLA_DK
    gain = g_ref[...]

    def chunk(c, carry):
        rows = pl.ds(pl.multiple_of(c * C, C), C)
        la = la_ref[rows, :]
        hi, lo = _split_bf16(la)
        b = _dot(tri, hi) + _dot(tri, lo)
        b_last = b[C - 1:C, :]
        b_mid = b[C // 2:C // 2 + 1, :]
        q = q_ref[rows, :]
        k = k_ref[rows, :]
        q_state = q * jnp.exp(b)
        q_in = q * jnp.exp(b - b_mid)
        k_in = k * jnp.exp(b_mid - b)
        k_out = k * jnp.exp(b_last - b)
        decay = jnp.exp(b_last)
        for p in range(GLA_HEADS // 2):
            ls = slice(p * LANES, (p + 1) * LANES)
            st = st_ref[p]
            o_inter = _dot_nt(_pair_blockdiag(q_state[:, ls], lane_lo), st.astype(BF16))
            sc = _dot_nt(_pair_blockdiag(q_in[:, ls], lane_lo), k_in[:, ls].astype(BF16))
            sc = jnp.where(causal, sc, 0.0).astype(BF16)
            k_out_p = k_out[:, ls].astype(BF16)
            upd = []
            for hh in range(2):
                head = 2 * p + hh
                vs = slice(head * GLA_DV, (head + 1) * GLA_DV)
                v = v_ref[rows, vs]
                o = o_inter[hh * C:(hh + 1) * C, :] + _dot(sc[hh * C:(hh + 1) * C, :], v)
                ms = jnp.mean(o * o, axis=-1, keepdims=True)
                o = o * lax.rsqrt(ms + EPS) * gain
                r = r_ref[rows, vs]
                o_ref[rows, vs] = (o * (r / (1.0 + jnp.exp(-r)))).astype(o_ref.dtype)
                upd.append(_dot_tn(v, k_out_p))
            st_ref[p] = st * decay[:, ls] + jnp.where(lane_lo_v, upd[0], upd[1])
        return carry

    lax.fori_loop(0, GLA_T // C, chunk, 0)


def _gla(gq, gk, la, gv, gr, gain):
    B, S, _ = gq.shape
    T = GLA_T
    tok = lambda n: pl.BlockSpec((None, T, n), lambda b, j: (b, j, 0))
    return pl.pallas_call(
        _gla_kernel,
        out_shape=jax.ShapeDtypeStruct((B, S, GLA_V), BF16),
        grid=(B, S // T),
        in_specs=[tok(GLA_QK), tok(GLA_QK), tok(GLA_QK), tok(GLA_V), tok(GLA_V),
                  _const_spec(gain.shape)],
        out_specs=tok(GLA_V),
        scratch_shapes=[pltpu.VMEM((GLA_HEADS // 2, GLA_DV, LANES), F32)],
        compiler_params=pltpu.CompilerParams(
            dimension_semantics=("parallel", "arbitrary"), vmem_limit_bytes=VMEM_LIMIT),
        name="gla",
    )(gq, gk, la, gv, gr, gain)


def _attn_kernel(q_ref, k_ref, kp_ref, v_ref, vp_ref, bias0_ref, bias_ref, o_ref, l_ref):
    BL = ATTN_BLOCK
    lane_lo = lax.broadcasted_iota(jnp.int32, (BL, LANES), 1) < ATTN_HEAD_DIM
    for t in range(ATTN_TQ // BL):
        cur = slice(t * BL, (t + 1) * BL)
        if t == 0:
            kk = jnp.concatenate([kp_ref[...], k_ref[cur, :]], axis=0)
            vv = jnp.concatenate([vp_ref[...], v_ref[cur, :]], axis=0)
            bias = bias0_ref
        else:
            both = slice((t - 1) * BL, (t + 1) * BL)
            kk = k_ref[both, :]
            vv = v_ref[both, :]
            bias = bias_ref
        qt = q_ref[cur, :]
        for p in range(ATTN_HEADS // 2):
            ls = slice(p * LANES, (p + 1) * LANES)
            s = _dot_nt(_pair_blockdiag(qt[:, ls], lane_lo), kk[:, ls]) + bias[p]
            m = jnp.max(s, axis=-1, keepdims=True)
            e = jnp.exp(s - m)
            den = jnp.sum(e, axis=-1, keepdims=True)
            o = _dot(e.astype(BF16), vv[:, ls]) / den
            lse = jnp.broadcast_to(m + jnp.log(den), o.shape)
            o_ref[cur, ls] = jnp.where(lane_lo, o[:BL], o[BL:])
            l_ref[cur, ls] = jnp.where(lane_lo, lse[:BL], lse[BL:])


def _attn_bias(dilation):
    span = DILATED_PAIRS[0][0] // DILATED_PAIRS[0][1]
    slopes = np.asarray([2.0 ** (-8.0 * (h + 1) / ATTN_HEADS) for h in range(ATTN_HEADS)],
                        dtype=np.float32)
    iq = np.arange(ATTN_BLOCK)[:, None]
    ik = np.arange(2 * ATTN_BLOCK)[None, :]
    rel = iq + ATTN_BLOCK - ik
    valid = (rel >= 0) & (rel <= span)
    out = np.empty((2, ATTN_HEADS, ATTN_BLOCK, 2 * ATTN_BLOCK), np.float32)
    for first in (0, 1):
        ok = valid & ((ik >= ATTN_BLOCK) if first == 0 else True)
        for h in range(ATTN_HEADS):
            alibi = -slopes[h] * (dilation * rel).astype(np.float32)
            out[first, h] = np.where(ok, alibi, np.float32(NEG_BIG))
    return out.reshape(2, ATTN_HEADS // 2, 2 * ATTN_BLOCK, 2 * ATTN_BLOCK)


def _attn(aq, ak, av, dilation):
    B, S, _ = aq.shape
    d = dilation
    L = S // d
    TQ = ATTN_TQ
    halo = TQ // ATTN_BLOCK
    view = lambda a: a.reshape(B, L, d * ATTN_DIM)
    bias = jnp.asarray(_attn_bias(d))
    main = pl.BlockSpec((None, TQ, ATTN_DIM), lambda b, r, j: (b, j, r))
    prev = pl.BlockSpec((None, ATTN_BLOCK, ATTN_DIM),
                        lambda b, r, j: (b, jnp.maximum(j * halo - 1, 0), r))
    bshape = (None,) + bias.shape[1:]
    o, l = pl.pallas_call(
        _attn_kernel,
        out_shape=[jax.ShapeDtypeStruct((B, L, d * ATTN_DIM), F32)] * 2,
        grid=(B, d, L // TQ),
        in_specs=[main, main, prev, main, prev,
                  pl.BlockSpec(bshape, lambda b, r, j: (jnp.minimum(j, 1), 0, 0, 0)),
                  pl.BlockSpec(bshape, lambda b, r, j: (1, 0, 0, 0))],
        out_specs=[main, main],
        compiler_params=pltpu.CompilerParams(
            dimension_semantics=("parallel", "parallel", "arbitrary"),
            vmem_limit_bytes=VMEM_LIMIT),
        name=f"attn_d{d}",
    )(view(aq), view(ak), view(ak), view(av), view(av), bias, bias)
    return o.reshape(B, S, ATTN_DIM), l.reshape(B, S, ATTN_DIM)


def _ffn_kernel(x_ref, yg_ref, o1_ref, l1_ref, o2_ref, l2_ref, o3_ref, l3_ref, mod_ref,
                wout_ref, g2_ref, wup_ref, cw_ref, cb_ref, wdn_ref, out_ref, carry_ref):
    tm = FFN_TM

    @pl.when(pl.program_id(1) == 0)
    def _():
        carry_ref[...] = jnp.zeros_like(carry_ref)

    l1, l2, l3 = l1_ref[...], l2_ref[...], l3_ref[...]
    m = jnp.maximum(jnp.maximum(l1, l2), l3)
    e1, e2, e3 = jnp.exp(l1 - m), jnp.exp(l2 - m), jnp.exp(l3 - m)
    y_att = (e1 * o1_ref[...] + e2 * o2_ref[...] + e3 * o3_ref[...]) / (e1 + e2 + e3)
    mixed = jnp.concatenate([yg_ref[...], y_att.astype(BF16)], axis=-1)
    x1 = x_ref[...] + mod_ref[2:3, :] * _dot(mixed, wout_ref[...])

    ms = jnp.mean(x1 * x1, axis=-1, keepdims=True)
    h = x1 * lax.rsqrt(ms + EPS) * g2_ref[...]
    hb = (h * (1.0 + mod_ref[4:5, :]) + mod_ref[3:4, :]).astype(BF16)

    rid = lax.broadcasted_iota(jnp.int32, (tm, FFN_CH), 0)

    def conv(u, c0):
        cols = slice(c0, c0 + FFN_CH)
        c6 = carry_ref[6:7, cols]
        c7 = carry_ref[7:8, cols]
        u1 = jnp.where(rid == 0, c7, pltpu.roll(u, 1, 0))
        u2 = jnp.where(rid == 0, c6, jnp.where(rid == 1, c7, pltpu.roll(u, 2, 0)))
        carry_ref[:, cols] = u[tm - 8:, :]
        return (cb_ref[:, cols] + cw_ref[0:1, cols] * u2 + cw_ref[1:2, cols] * u1
                + cw_ref[2:3, cols] * u)

    acc = jnp.zeros((tm, D_MODEL), F32)
    for c in range(D_FF // FFN_CH):
        g0 = c * FFN_CH
        v0 = D_FF + c * FFN_CH
        ug = conv(_dot(hb, wup_ref[:, g0:g0 + FFN_CH]), g0)
        uv = conv(_dot(hb, wup_ref[:, v0:v0 + FFN_CH]), v0)
        act = (ug / (1.0 + jnp.exp(-ug))) * uv
        acc = acc + _dot(act.astype(BF16), wdn_ref[g0:g0 + FFN_CH, :])
    out_ref[...] = x1 + mod_ref[5:6, :] * acc


def _ffn(x, yg, branches, mod3, w_out, g2, w_up, conv_w, conv_b, w_dn):
    B, S, _ = x.shape
    tm = FFN_TM
    tok = lambda n: pl.BlockSpec((None, tm, n), lambda b, i: (b, i, 0))
    flat = [a for ol in branches for a in ol]
    return pl.pallas_call(
        _ffn_kernel,
        out_shape=jax.ShapeDtypeStruct((B, S, D_MODEL), F32),
        grid=(B, S // tm),
        in_specs=[tok(D_MODEL), tok(GLA_V)] + [tok(ATTN_DIM)] * 6 + [
            pl.BlockSpec((None, N_MOD, D_MODEL), lambda b, i: (b, 0, 0)),
            _const_spec(w_out.shape), _const_spec(g2.shape), _const_spec(w_up.shape),
            _const_spec(conv_w.shape), _const_spec(conv_b.shape), _const_spec(w_dn.shape)],
        out_specs=tok(D_MODEL),
        scratch_shapes=[pltpu.VMEM((8, 2 * D_FF), F32)],
        compiler_params=pltpu.CompilerParams(
            dimension_semantics=("parallel", "arbitrary"), vmem_limit_bytes=VMEM_LIMIT),
        name="ffn",
    )(x, yg, *flat, mod3, w_out, g2, w_up, conv_w, conv_b, w_dn)


def _head_ones():
    blk = np.kron(np.eye(ATTN_HEADS, dtype=np.float32),
                  np.ones((ATTN_HEAD_DIM, ATTN_HEAD_DIM), np.float32))
    return jnp.asarray(blk, dtype=BF16)


def kernel(x, c, w_ada, b_ada, norm1_g, w_in, gla_w_gate, gla_b_gate, gla_norm_g, q_norm_g,
           k_norm_g, w_out, norm2_g, w_up, conv_w, conv_b, w_down):
    B = x.shape[0]
    depth = w_ada.shape[0]
    ones_blk = _head_ones()
    glr0 = 2 * GLA_QK + 2 * GLA_V
    for l in range(depth):
        mod3 = _ada(c, w_ada[l], b_ada[l]).reshape(B, N_MOD, D_MODEL)
        w = w_in[l]
        w_main = jnp.concatenate([w[:, :glr0], w[:, glr0 + GLA_GATE_RANK:]], axis=1).astype(BF16)
        w_glr = jnp.pad(w[:, glr0:glr0 + GLA_GATE_RANK],
                        ((0, 0), (0, LANES - GLA_GATE_RANK))).astype(BF16)
        w_gate = jnp.pad(gla_w_gate[l], ((0, LANES - GLA_GATE_RANK), (0, 0))).astype(BF16)
        gq, gk, la, gv, gr, aq, ak, av = _inproj(
            x, mod3, norm1_g[l].reshape(1, -1), w_main, w_glr, w_gate,
            gla_b_gate[l].reshape(1, -1), ones_blk,
            jnp.tile(q_norm_g[l], ATTN_HEADS).reshape(1, -1),
            jnp.tile(k_norm_g[l], ATTN_HEADS).reshape(1, -1))
        yg = _gla(gq, gk, la, gv, gr, gla_norm_g[l].reshape(1, -1))
        branches = [_attn(aq, ak, av, d) for _, d in DILATED_PAIRS]
        x = _ffn(x, yg, branches, mod3, w_out[l].astype(BF16), norm2_g[l].reshape(1, -1),
                 w_up[l].astype(BF16), conv_w[l], conv_b[l].reshape(1, -1),
                 w_down[l].astype(BF16))
    return x
```

```python
import functools

import jax
import jax.numpy as jnp
import numpy as np
from jax import lax
from jax.experimental import pallas as pl
from jax.experimental.pallas import tpu as pltpu

F32 = jnp.float32
BF16 = jnp.bfloat16

D_MODEL = 1024
GLA_HEADS = 4
GLA_DK = 64
GLA_DV = 128
GLA_GATE_RANK = 16
GLA_GATE_TAU = 16.0
GLA_CHUNK = 64
GLA_QK = GLA_HEADS * GLA_DK
GLA_V = GLA_HEADS * GLA_DV
ATTN_HEADS = 8
ATTN_HEAD_DIM = 64
ATTN_DIM = ATTN_HEADS * ATTN_HEAD_DIM
DILATED_PAIRS = ((128, 1), (512, 4), (2048, 16))
ATTN_BLOCK = 128
D_FF = 2816
CONV_WIDTH = 3
N_MOD = 6
EPS = 1e-6

LANES = 128
MXU_DEPTH = 256
NEG_BIG = -1e30

INPROJ_TM = 512
GLA_T = 512
ATTN_SPAN = 2048
ATTN_UNROLL = 4
ATTN_MERGE_ROWS = 64
FFN_TM = 512
FFN_CH = 256
VMEM_LIMIT = 56 * 1024 * 1024


def _const_spec(shape):
    nd = len(shape)
    return pl.BlockSpec(shape, lambda *_: (0,) * nd, pipeline_mode=pl.Buffered(1))


def _dot(a, b):
    return jnp.dot(a, b, preferred_element_type=F32)


def _dot_nt(a, b):
    return lax.dot_general(a, b, (((1,), (1,)), ((), ())), preferred_element_type=F32)


def _dot_tn(a, b):
    return lax.dot_general(a, b, (((0,), (0,)), ((), ())), preferred_element_type=F32)


def _split_bf16(x):
    hi = x.astype(BF16)
    lo = (x - hi.astype(F32)).astype(BF16)
    return hi, lo


def _ada_kernel(c_ref, w_ref, b_ref, o_ref):
    c = c_ref[...]
    cond = c / (1.0 + jnp.exp(-c))
    o_ref[...] = _dot(cond.astype(BF16), w_ref[...].astype(BF16)) + b_ref[...]


def _ada(c, w_ada, b_ada):
    B = c.shape[0]
    n = w_ada.shape[1]
    bn = D_MODEL
    return pl.pallas_call(
        _ada_kernel,
        out_shape=jax.ShapeDtypeStruct((B, n), F32),
        grid=(n // bn,),
        in_specs=[
            pl.BlockSpec((B, D_MODEL), lambda j: (0, 0)),
            pl.BlockSpec((D_MODEL, bn), lambda j: (0, j)),
            pl.BlockSpec((1, bn), lambda j: (0, j)),
        ],
        out_specs=pl.BlockSpec((B, bn), lambda j: (0, j)),
        name="ada",
    )(c, w_ada, b_ada.reshape(1, n))


def _inproj_kernel(x_ref, mod_ref, g1_ref, w_ref, wglr_ref, wgate_ref, bgate_ref, ones_ref,
                   qg_ref, kg_ref,
                   gq_ref, gk_ref, la_ref, gv_ref, gr_ref, aq_ref, ak_ref, av_ref):
    x = x_ref[...]
    ms = jnp.mean(x * x, axis=-1, keepdims=True)
    y = x * lax.rsqrt(ms + EPS) * g1_ref[...]
    h = y * (1.0 + mod_ref[1:2, :]) + mod_ref[0:1, :]
    hb = h.astype(BF16)

    def proj(c0, n):
        return _dot(hb, w_ref[:, c0:c0 + n])

    def head_norm(p, gain):
        gw = ones_ref.shape[0]
        hi, lo = _split_bf16(p * p)
        ssum = jnp.concatenate(
            [_dot(hi[:, c:c + gw], ones_ref[...]) + _dot(lo[:, c:c + gw], ones_ref[...])
             for c in range(0, ATTN_DIM, gw)], axis=-1)
        return p * lax.rsqrt(ssum * (1.0 / ATTN_HEAD_DIM) + EPS) * gain

    a0 = 2 * GLA_QK + 2 * GLA_V
    glr = _dot(hb, wglr_ref[...])
    aq_raw = proj(a0, ATTN_DIM)
    ak_raw = proj(a0 + ATTN_DIM, ATTN_DIM)
    gq_ref[...] = proj(0, GLA_QK) * (GLA_DK ** -0.5)
    gk_ref[...] = proj(GLA_QK, GLA_QK)
    gv_ref[...] = proj(2 * GLA_QK, GLA_V).astype(BF16)
    gr_ref[...] = proj(2 * GLA_QK + GLA_V, GLA_V)
    av_ref[...] = proj(a0 + 2 * ATTN_DIM, ATTN_DIM).astype(BF16)

    z = _dot(glr.astype(BF16), wgate_ref[...]) + bgate_ref[...]
    log_sig = jnp.minimum(z, 0.0) - jnp.log1p(jnp.exp(-jnp.abs(z)))
    la_ref[...] = log_sig * (1.0 / GLA_GATE_TAU)
    aq_ref[...] = (head_norm(aq_raw, qg_ref[...]) * (ATTN_HEAD_DIM ** -0.5)).astype(BF16)
    ak_ref[...] = head_norm(ak_raw, kg_ref[...]).astype(BF16)


def _inproj(x, mod3, g1, w_main, w_glr, w_gate, b_gate, ones_blk, qg, kg):
    B, S, _ = x.shape
    tm = INPROJ_TM
    tok = lambda n: pl.BlockSpec((None, tm, n), lambda b, i: (b, i, 0))
    out_dims = [(GLA_QK, F32), (GLA_QK, F32), (GLA_QK, F32), (GLA_V, BF16), (GLA_V, F32),
                (ATTN_DIM, BF16), (ATTN_DIM, BF16), (ATTN_DIM, BF16)]
    return pl.pallas_call(
        _inproj_kernel,
        out_shape=[jax.ShapeDtypeStruct((B, S, n), dt) for n, dt in out_dims],
        grid=(B, S // tm),
        in_specs=[
            tok(D_MODEL),
            pl.BlockSpec((None, N_MOD, D_MODEL), lambda b, i: (b, 0, 0)),
            _const_spec(g1.shape), _const_spec(w_main.shape), _const_spec(w_glr.shape),
            _const_spec(w_gate.shape), _const_spec(b_gate.shape), _const_spec(ones_blk.shape),
            _const_spec(qg.shape), _const_spec(kg.shape),
        ],
        out_specs=[tok(n) for n, _ in out_dims],
        compiler_params=pltpu.CompilerParams(
            dimension_semantics=("parallel", "parallel"), vmem_limit_bytes=VMEM_LIMIT),
        name="inproj",
    )(x, mod3, g1, w_main, w_glr, w_gate, b_gate, ones_blk, qg, kg)


def _pair_blockdiag(a, lane_lo):
    zero = jnp.zeros_like(a)
    return jnp.concatenate([jnp.where(lane_lo, a, zero), jnp.where(lane_lo, zero, a)],
                           axis=0).astype(BF16)


def _gla_kernel(q_ref, k_ref, la_ref, v_ref, r_ref, g_ref, tri_ref, o_ref, st_ref):
    C = GLA_CHUNK
    chunks = range(GLA_T // C)
    pairs = range(GLA_HEADS // 2)
    lanes = lambda p: slice(p * LANES, (p + 1) * LANES)
    rows = lambda c: slice(c * C, (c + 1) * C)
    vals = lambda h: slice(h * GLA_DV, (h + 1) * GLA_DV)

    @pl.when(pl.program_id(1) == 0)
    def _():
        st_ref[...] = jnp.zeros_like(st_ref)

    row2 = lax.broadcasted_iota(jnp.int32, (2 * C, C), 0)
    col2 = lax.broadcasted_iota(jnp.int32, (2 * C, C), 1)
    causal = col2 <= jnp.where(row2 >= C, row2 - C, row2)
    lane_lo = lax.broadcasted_iota(jnp.int32, (C, LANES), 1) < GLA_DK
    lane_lo_v = lax.broadcasted_iota(jnp.int32, (GLA_DV, LANES), 1) < GLA_DK
    gain = g_ref[...]

    hi, lo = _split_bf16(la_ref[...])
    b_all = _dot(tri_ref[...], hi) + _dot(tri_ref[...], lo)

    q_state, q_in, k_in, k_out, decay = [], [], [], [], []
    for c in chunks:
        b = b_all[rows(c), :]
        b_last = b[C - 1:C, :]
        b_mid = b[C // 2:C // 2 + 1, :]
        q = q_ref[rows(c), :]
        k = k_ref[rows(c), :]
        qs = q * jnp.exp(b)
        qi = q * jnp.exp(b - b_mid)
        q_state.append([_pair_blockdiag(qs[:, lanes(p)], lane_lo) for p in pairs])
        q_in.append([_pair_blockdiag(qi[:, lanes(p)], lane_lo) for p in pairs])
        k_in.append((k * jnp.exp(b_mid - b)).astype(BF16))
        k_out.append((k * jnp.exp(b_last - b)).astype(BF16))
        decay.append(jnp.exp(b_last))

    scores, update = [], []
    for c in chunks:
        scores.append([_dot_nt(q_in[c][p], k_in[c][:, lanes(p)]) for p in pairs])
        update.append([[_dot_tn(v_ref[rows(c), vals(2 * p + hh)], k_out[c][:, lanes(p)])
                        for hh in range(2)] for p in pairs])
    scores = [[jnp.where(causal, s, 0.0).astype(BF16) for s in sc] for sc in scores]

    state = [st_ref[p] for p in pairs]
    seen = []
    for c in chunks:
        seen.append(state)
        state = [state[p] * decay[c][:, lanes(p)]
                 + jnp.where(lane_lo_v, update[c][p][0], update[c][p][1]) for p in pairs]
    for p in pairs:
        st_ref[p] = state[p]

    for c in chunks:
        for p in pairs:
            o_inter = _dot_nt(q_state[c][p], seen[c][p].astype(BF16))
            for hh in range(2):
                vs = vals(2 * p + hh)
                half = slice(hh * C, (hh + 1) * C)
                o = o_inter[half, :] + _dot(scores[c][p][half, :], v_ref[rows(c), vs])
                ms = jnp.mean(o * o, axis=-1, keepdims=True)
                o = o * lax.rsqrt(ms + EPS) * gain
                r = r_ref[rows(c), vs]
                o_ref[rows(c), vs] = (o * (r / (1.0 + jnp.exp(-r)))).astype(o_ref.dtype)


def _chunk_prefix_matrix():
    i = np.arange(GLA_T)
    same = (i[:, None] // GLA_CHUNK) == (i[None, :] // GLA_CHUNK)
    return jnp.asarray(same & (i[None, :] <= i[:, None]), dtype=BF16)


def _gla(gq, gk, la, gv, gr, gain):
    B, S, _ = gq.shape
    T = GLA_T
    tri = _chunk_prefix_matrix()
    tok = lambda n: pl.BlockSpec((None, T, n), lambda b, j: (b, j, 0))
    return pl.pallas_call(
        _gla_kernel,
        out_shape=jax.ShapeDtypeStruct((B, S, GLA_V), BF16),
        grid=(B, S // T),
        in_specs=[tok(GLA_QK), tok(GLA_QK), tok(GLA_QK), tok(GLA_V), tok(GLA_V),
                  _const_spec(gain.shape), _const_spec(tri.shape)],
        out_specs=tok(GLA_V),
        scratch_shapes=[pltpu.VMEM((GLA_HEADS // 2, GLA_DV, LANES), F32)],
        compiler_params=pltpu.CompilerParams(
            dimension_semantics=("parallel", "arbitrary"), vmem_limit_bytes=VMEM_LIMIT),
        name="gla",
    )(gq, gk, la, gv, gr, gain, tri)


def _attn_kernel(q_ref, k_ref, kp_ref, v_ref, vp_ref, bias_ref, y_ref, qf, kf, vf, og, mg, dg):
    SP, BL = ATTN_SPAN, ATTN_BLOCK
    first_span = pl.program_id(1) == 0
    qf[...] = q_ref[...].astype(F32)
    kf[0:SP, :] = kp_ref[...].astype(F32)
    kf[SP:, :] = k_ref[...].astype(F32)
    vf[0:SP, :] = vp_ref[...].astype(F32)
    vf[SP:, :] = v_ref[...].astype(F32)
    lane_lo = lax.broadcasted_iota(jnp.int32, (BL, LANES), 1) < ATTN_HEAD_DIM
    ones = jnp.ones((2 * BL, LANES), BF16)

    tiles = []
    for g, (_, d) in enumerate(DILATED_PAIRS):
        for r in range(d):
            for n in range(SP // (BL * d)):
                qs = r + n * (d * BL)
                ks = qs + (SP - d * BL)
                tiles.append((g, d, qs, ks, n == 0))
    groups = [tiles[i:i + ATTN_UNROLL] for i in range(0, len(tiles), ATTN_UNROLL)]

    def score_stage(group):
        return [_dot_nt(_pair_blockdiag(qf[pl.ds(qs, BL, stride=d), :], lane_lo),
                        kf[pl.ds(ks, 2 * BL, stride=d), :].astype(BF16))
                for _, d, qs, ks, _ in group]

    def softmax_stage(group, scores):
        out = []
        for (g, _, _, _, seq_start), s in zip(group, scores):
            sel = jnp.where(first_span, 0, 1) if seq_start else 1
            s = s + bias_ref[g, sel]
            m = jnp.max(s, axis=-1, keepdims=True)
            out.append((m, jnp.exp(s - m).astype(BF16)))
        return out

    def value_stage(group, stats):
        accs = [_dot(e, jnp.concatenate(
                    [vf[pl.ds(ks, 2 * BL, stride=d), :].astype(BF16), ones], axis=1))
                for (_, d, _, ks, _), (_, e) in zip(group, stats)]
        for (g, d, qs, _, _), (m, _), acc in zip(group, stats, accs):
            rows = pl.ds(qs, BL, stride=d)
            og[g, rows, :] = jnp.where(lane_lo, acc[:BL, :LANES], acc[BL:, :LANES])
            mg[g, rows, :] = jnp.where(lane_lo, m[:BL], m[BL:])
            dg[g, rows, :] = jnp.where(lane_lo, acc[:BL, LANES:], acc[BL:, LANES:])

    scores = score_stage(groups[0])
    for i, group in enumerate(groups):
        ahead = score_stage(groups[i + 1]) if i + 1 < len(groups) else None
        value_stage(group, softmax_stage(group, scores))
        scores = ahead

    def merge(c, carry):
        rows = pl.ds(pl.multiple_of(c * ATTN_MERGE_ROWS, ATTN_MERGE_ROWS), ATTN_MERGE_ROWS)
        ms = [mg[g, rows, :] for g in range(len(DILATED_PAIRS))]
        top = functools.reduce(jnp.maximum, ms)
        ws = [jnp.exp(mi - top) for mi in ms]
        num = sum(w * og[g, rows, :] for g, w in enumerate(ws))
        den = sum(w * dg[g, rows, :] for g, w in enumerate(ws))
        y_ref[rows, :] = (num / den).astype(y_ref.dtype)
        return carry

    lax.fori_loop(0, SP // ATTN_MERGE_ROWS, merge, 0)


def _attn_bias():
    span = DILATED_PAIRS[0][0] // DILATED_PAIRS[0][1]
    slopes = np.asarray([2.0 ** (-8.0 * (h + 1) / ATTN_HEADS) for h in range(ATTN_HEADS)],
                        dtype=np.float32)
    iq = np.arange(ATTN_BLOCK)[:, None]
    ik = np.arange(2 * ATTN_BLOCK)[None, :]
    rel = iq + ATTN_BLOCK - ik
    valid = (rel >= 0) & (rel <= span)
    out = np.empty((len(DILATED_PAIRS), 2, ATTN_HEADS, ATTN_BLOCK, 2 * ATTN_BLOCK), np.float32)
    for g, (_, d) in enumerate(DILATED_PAIRS):
        for first in (0, 1):
            ok = valid & ((ik >= ATTN_BLOCK) if first == 0 else True)
            for h in range(ATTN_HEADS):
                alibi = -slopes[h] * (d * rel).astype(np.float32)
                out[g, first, h] = np.where(ok, alibi, np.float32(NEG_BIG))
    return out.reshape(len(DILATED_PAIRS), 2, ATTN_HEADS // 2, 2 * ATTN_BLOCK, 2 * ATTN_BLOCK)


def _attn(aq, ak, av):
    B, S, _ = aq.shape
    SP = ATTN_SPAN
    G = len(DILATED_PAIRS)
    bias = jnp.asarray(_attn_bias())
    cur = pl.BlockSpec((None, SP, LANES), lambda b, j, p: (b, j, p))
    prev = pl.BlockSpec((None, SP, LANES), lambda b, j, p: (b, jnp.maximum(j - 1, 0), p))
    return pl.pallas_call(
        _attn_kernel,
        out_shape=jax.ShapeDtypeStruct((B, S, ATTN_DIM), BF16),
        grid=(B, S // SP, ATTN_HEADS // 2),
        in_specs=[cur, cur, prev, cur, prev,
                  pl.BlockSpec((G, 2, None, 2 * ATTN_BLOCK, 2 * ATTN_BLOCK),
                               lambda b, j, p: (0, 0, p, 0, 0))],
        out_specs=cur,
        scratch_shapes=[pltpu.VMEM((SP, LANES), F32), pltpu.VMEM((2 * SP, LANES), F32),
                        pltpu.VMEM((2 * SP, LANES), F32), pltpu.VMEM((G, SP, LANES), F32),
                        pltpu.VMEM((G, SP, LANES), F32), pltpu.VMEM((G, SP, LANES), F32)],
        compiler_params=pltpu.CompilerParams(
            dimension_semantics=("parallel", "parallel", "parallel"),
            vmem_limit_bytes=VMEM_LIMIT),
        name="attn",
    )(aq, ak, ak, av, av, bias)


def _ffn_kernel(x_ref, yg_ref, ya_ref, mod_ref, wout_ref, g2_ref, wup_ref, cw_ref, cb_ref,
                wdn_ref, out_ref, carry_ref, act_ref):
    tm = FFN_TM

    @pl.when(pl.program_id(1) == 0)
    def _():
        carry_ref[...] = jnp.zeros_like(carry_ref)

    mixed = jnp.concatenate([yg_ref[...], ya_ref[...]], axis=-1)
    x1 = x_ref[...] + mod_ref[2:3, :] * _dot(mixed, wout_ref[...])

    ms = jnp.mean(x1 * x1, axis=-1, keepdims=True)
    h = x1 * lax.rsqrt(ms + EPS) * g2_ref[...]
    hb = (h * (1.0 + mod_ref[4:5, :]) + mod_ref[3:4, :]).astype(BF16)

    rid = lax.broadcasted_iota(jnp.int32, (8, FFN_CH), 0)

    def conv(u, c0):
        cols = slice(c0, c0 + FFN_CH)
        c6 = carry_ref[6:7, cols]
        c7 = carry_ref[7:8, cols]
        r1 = pltpu.roll(u, 1, 0)
        r2 = pltpu.roll(u, 2, 0)
        u1 = jnp.concatenate([jnp.where(rid == 0, c7, r1[:8])] + [r1[8:]], axis=0)
        u2 = jnp.concatenate(
            [jnp.where(rid == 0, c6, jnp.where(rid == 1, c7, r2[:8]))] + [r2[8:]], axis=0)
        carry_ref[:, cols] = u[tm - 8:, :]
        return (cb_ref[:, cols] + cw_ref[0:1, cols] * u2 + cw_ref[1:2, cols] * u1
                + cw_ref[2:3, cols] * u)

    for c in range(D_FF // FFN_CH):
        g0 = c * FFN_CH
        v0 = D_FF + c * FFN_CH
        ug = conv(_dot(hb, wup_ref[:, g0:g0 + FFN_CH]), g0)
        uv = conv(_dot(hb, wup_ref[:, v0:v0 + FFN_CH]), v0)
        act_ref[:, g0:g0 + FFN_CH] = ((ug / (1.0 + jnp.exp(-ug))) * uv).astype(BF16)
    out_ref[...] = x1 + mod_ref[5:6, :] * _dot(act_ref[...], wdn_ref[...])


def _ffn(x, yg, ya, mod3, w_out, g2, w_up, conv_w, conv_b, w_dn):
    B, S, _ = x.shape
    tm = FFN_TM
    tok = lambda n: pl.BlockSpec((None, tm, n), lambda b, i: (b, i, 0))
    return pl.pallas_call(
        _ffn_kernel,
        out_shape=jax.ShapeDtypeStruct((B, S, D_MODEL), F32),
        grid=(B, S // tm),
        in_specs=[tok(D_MODEL), tok(GLA_V), tok(ATTN_DIM),
                  pl.BlockSpec((None, N_MOD, D_MODEL), lambda b, i: (b, 0, 0)),
                  _const_spec(w_out.shape), _const_spec(g2.shape), _const_spec(w_up.shape),
                  _const_spec(conv_w.shape), _const_spec(conv_b.shape), _const_spec(w_dn.shape)],
        out_specs=tok(D_MODEL),
        scratch_shapes=[pltpu.VMEM((8, 2 * D_FF), F32), pltpu.VMEM((tm, D_FF), BF16)],
        compiler_params=pltpu.CompilerParams(
            dimension_semantics=("parallel", "arbitrary"), vmem_limit_bytes=VMEM_LIMIT),
        name="ffn",
    )(x, yg, ya, mod3, w_out, g2, w_up, conv_w, conv_b, w_dn)


def _head_ones():
    heads = MXU_DEPTH // ATTN_HEAD_DIM
    blk = np.kron(np.eye(heads, dtype=np.float32),
                  np.ones((ATTN_HEAD_DIM, ATTN_HEAD_DIM), np.float32))
    return jnp.asarray(blk, dtype=BF16)


def kernel(x, c, w_ada, b_ada, norm1_g, w_in, gla_w_gate, gla_b_gate, gla_norm_g, q_norm_g,
           k_norm_g, w_out, norm2_g, w_up, conv_w, conv_b, w_down):
    B = x.shape[0]
    depth = w_ada.shape[0]
    ones_blk = _head_ones()
    glr0 = 2 * GLA_QK + 2 * GLA_V
    for l in range(depth):
        mod3 = _ada(c, w_ada[l], b_ada[l]).reshape(B, N_MOD, D_MODEL)
        w = w_in[l]
        w_main = jnp.concatenate([w[:, :glr0], w[:, glr0 + GLA_GATE_RANK:]], axis=1).astype(BF16)
        w_glr = jnp.pad(w[:, glr0:glr0 + GLA_GATE_RANK],
                        ((0, 0), (0, LANES - GLA_GATE_RANK))).astype(BF16)
        w_gate = jnp.pad(gla_w_gate[l], ((0, LANES - GLA_GATE_RANK), (0, 0))).astype(BF16)
        gq, gk, la, gv, gr, aq, ak, av = _inproj(
            x, mod3, norm1_g[l].reshape(1, -1), w_main, w_glr, w_gate,
            gla_b_gate[l].reshape(1, -1), ones_blk,
            jnp.tile(q_norm_g[l], ATTN_HEADS).reshape(1, -1),
            jnp.tile(k_norm_g[l], ATTN_HEADS).reshape(1, -1))
        yg = _gla(gq, gk, la, gv, gr, gla_norm_g[l].reshape(1, -1))
        ya = _attn(aq, ak, av)
        x = _ffn(x, yg, ya, mod3, w_out[l].astype(BF16), norm2_g[l].reshape(1, -1),
                 w_up[l].astype(BF16), conv_w[l], conv_b[l].reshape(1, -1),
                 w_down[l].astype(BF16))
    return x
```

```python
import functools

import jax
import jax.numpy as jnp
import numpy as np
from jax import lax
from jax.experimental import pallas as pl
from jax.experimental.pallas import tpu as pltpu

F32 = jnp.float32
BF16 = jnp.bfloat16

D_MODEL = 1024
GLA_HEADS = 4
GLA_DK = 64
GLA_DV = 128
GLA_GATE_RANK = 16
GLA_GATE_TAU = 16.0
GLA_CHUNK = 64
GLA_QK = GLA_HEADS * GLA_DK
GLA_V = GLA_HEADS * GLA_DV
ATTN_HEADS = 8
ATTN_HEAD_DIM = 64
ATTN_DIM = ATTN_HEADS * ATTN_HEAD_DIM
DILATED_PAIRS = ((128, 1), (512, 4), (2048, 16))
ATTN_BLOCK = 128
D_FF = 2816
CONV_WIDTH = 3
N_MOD = 6
EPS = 1e-6

LANES = 128
MXU_DEPTH = 256
NEG_BIG = -1e30
LOG2E = 1.4426950408889634

INPROJ_TM = 512
GLA_T = 512
GLA_SAFE_RANGE = 60.0
ATTN_SPAN = 2048
ATTN_UNROLL = 4
ATTN_MERGE_ROWS = 64
FFN_TM = 1024
FFN_SUB = 512
FFN_CH = 256
VMEM_LIMIT = 56 * 1024 * 1024


def _const_spec(shape):
    nd = len(shape)
    return pl.BlockSpec(shape, lambda *_: (0,) * nd, pipeline_mode=pl.Buffered(1))


def _dot(a, b):
    return jnp.dot(a, b, preferred_element_type=F32)


def _dot_nt(a, b):
    return lax.dot_general(a, b, (((1,), (1,)), ((), ())), preferred_element_type=F32)


def _dot_tn(a, b):
    return lax.dot_general(a, b, (((0,), (0,)), ((), ())), preferred_element_type=F32)


def _split_bf16(x):
    hi = x.astype(BF16)
    lo = (x - hi.astype(F32)).astype(BF16)
    return hi, lo


def _ada_kernel(c_ref, w_ref, b_ref, o_ref):
    c = c_ref[...]
    cond = c / (1.0 + jnp.exp(-c))
    o_ref[...] = _dot(cond.astype(BF16), w_ref[...].astype(BF16)) + b_ref[...]


def _ada(c, w_ada, b_ada):
    B = c.shape[0]
    n = w_ada.shape[1]
    bn = D_MODEL
    return pl.pallas_call(
        _ada_kernel,
        out_shape=jax.ShapeDtypeStruct((B, n), F32),
        grid=(n // bn,),
        in_specs=[
            pl.BlockSpec((B, D_MODEL), lambda j: (0, 0)),
            pl.BlockSpec((D_MODEL, bn), lambda j: (0, j)),
            pl.BlockSpec((1, bn), lambda j: (0, j)),
        ],
        out_specs=pl.BlockSpec((B, bn), lambda j: (0, j)),
        name="ada",
    )(c, w_ada, b_ada.reshape(1, n))


def _inproj_kernel(x_ref, mod_ref, g1_ref, w_ref, wglr_ref, wgate_ref, bgate_ref, ones_ref,
                   qg_ref, kg_ref,
                   gq_ref, gk_ref, la_ref, gv_ref, gr_ref, aq_ref, ak_ref, av_ref):
    x = x_ref[...]
    ms = jnp.mean(x * x, axis=-1, keepdims=True)
    y = x * lax.rsqrt(ms + EPS) * g1_ref[...]
    h = y * (1.0 + mod_ref[1:2, :]) + mod_ref[0:1, :]
    hb = h.astype(BF16)

    def proj(c0, n):
        return _dot(hb, w_ref[:, c0:c0 + n])

    def head_norm(p, gain):
        gw = ones_ref.shape[0]
        hi, lo = _split_bf16(p * p)
        ssum = jnp.concatenate(
            [_dot(hi[:, c:c + gw], ones_ref[...]) + _dot(lo[:, c:c + gw], ones_ref[...])
             for c in range(0, ATTN_DIM, gw)], axis=-1)
        return p * lax.rsqrt(ssum * (1.0 / ATTN_HEAD_DIM) + EPS) * gain

    a0 = 2 * GLA_QK + 2 * GLA_V
    glr = _dot(hb, wglr_ref[...])
    aq_raw = proj(a0, ATTN_DIM)
    ak_raw = proj(a0 + ATTN_DIM, ATTN_DIM)
    gq_ref[...] = proj(0, GLA_QK) * (GLA_DK ** -0.5)
    gk_ref[...] = proj(GLA_QK, GLA_QK)
    gv_ref[...] = proj(2 * GLA_QK, GLA_V).astype(BF16)
    gr_ref[...] = proj(2 * GLA_QK + GLA_V, GLA_V)
    av_ref[...] = proj(a0 + 2 * ATTN_DIM, ATTN_DIM).astype(BF16)

    z = _dot(glr.astype(BF16), wgate_ref[...]) + bgate_ref[...]
    log_sig = jnp.minimum(z, 0.0) - jnp.log1p(jnp.exp(-jnp.abs(z)))
    la_ref[...] = log_sig * (1.0 / GLA_GATE_TAU)
    aq_ref[...] = (head_norm(aq_raw, qg_ref[...]) * (ATTN_HEAD_DIM ** -0.5 * LOG2E)).astype(BF16)
    ak_ref[...] = head_norm(ak_raw, kg_ref[...]).astype(BF16)


def _inproj(x, mod3, g1, w_main, w_glr, w_gate, b_gate, ones_blk, qg, kg):
    B, S, _ = x.shape
    tm = INPROJ_TM
    tok = lambda n: pl.BlockSpec((None, tm, n), lambda b, i: (b, i, 0))
    out_dims = [(GLA_QK, F32), (GLA_QK, F32), (GLA_QK, F32), (GLA_V, BF16), (GLA_V, F32),
                (ATTN_DIM, BF16), (ATTN_DIM, BF16), (ATTN_DIM, BF16)]
    return pl.pallas_call(
        _inproj_kernel,
        out_shape=[jax.ShapeDtypeStruct((B, S, n), dt) for n, dt in out_dims],
        grid=(B, S // tm),
        in_specs=[
            tok(D_MODEL),
            pl.BlockSpec((None, N_MOD, D_MODEL), lambda b, i: (b, 0, 0)),
            _const_spec(g1.shape), _const_spec(w_main.shape), _const_spec(w_glr.shape),
            _const_spec(w_gate.shape), _const_spec(b_gate.shape), _const_spec(ones_blk.shape),
            _const_spec(qg.shape), _const_spec(kg.shape),
        ],
        out_specs=[tok(n) for n, _ in out_dims],
        compiler_params=pltpu.CompilerParams(
            dimension_semantics=("parallel", "parallel"), vmem_limit_bytes=VMEM_LIMIT),
        name="inproj",
    )(x, mod3, g1, w_main, w_glr, w_gate, b_gate, ones_blk, qg, kg)


def _pair_blockdiag(a, lane_lo):
    zero = jnp.zeros_like(a)
    return jnp.concatenate([jnp.where(lane_lo, a, zero), jnp.where(lane_lo, zero, a)],
                           axis=0).astype(BF16)


def _gla_kernel(q_ref, k_ref, la_ref, v_ref, r_ref, g_ref, tri_ref, o_ref,
                st_ref, st0_ref, raw_ref):
    C = GLA_CHUNK
    chunks = range(GLA_T // C)
    pairs = range(GLA_HEADS // 2)
    lanes = lambda p: slice(p * LANES, (p + 1) * LANES)
    rows = lambda c: slice(c * C, (c + 1) * C)
    vals = lambda h: slice(h * GLA_DV, (h + 1) * GLA_DV)

    @pl.when(pl.program_id(1) == 0)
    def _():
        st_ref[...] = jnp.zeros_like(st_ref)

    st0_ref[...] = st_ref[...]

    def finish(o, r, rsel, vs):
        ms = jnp.mean(o * o, axis=-1, keepdims=True)
        o = o * lax.rsqrt(ms + EPS) * g_ref[...]
        o_ref[rsel, vs] = (o * (r / (1.0 + jnp.exp(-r)))).astype(o_ref.dtype)

    row2 = lax.broadcasted_iota(jnp.int32, (2 * C, C), 0)
    col2 = lax.broadcasted_iota(jnp.int32, (2 * C, C), 1)
    causal = col2 <= jnp.where(row2 >= C, row2 - C, row2)
    lane_lo = lax.broadcasted_iota(jnp.int32, (C, LANES), 1) < GLA_DK
    lane_lo_v = lax.broadcasted_iota(jnp.int32, (GLA_DV, LANES), 1) < GLA_DK

    hi, lo = _split_bf16(la_ref[...])
    cum = [_dot(tri_ref[...], hi[rows(c), :]) + _dot(tri_ref[...], lo[rows(c), :])
           for c in chunks]

    q_state, q_in, k_in, k_out, decay = [], [], [], [], []
    for c in chunks:
        b = cum[c]
        b_last = b[C - 1:C, :]
        b_mid = b[C // 2:C // 2 + 1, :]
        q = q_ref[rows(c), :]
        k = k_ref[rows(c), :]
        qs = q * jnp.exp(b)
        qi = q * jnp.exp(b - b_mid)
        q_state.append([_pair_blockdiag(qs[:, lanes(p)], lane_lo) for p in pairs])
        q_in.append([_pair_blockdiag(qi[:, lanes(p)], lane_lo) for p in pairs])
        k_in.append((k * jnp.exp(b_mid - b)).astype(BF16))
        k_out.append((k * jnp.exp(b_last - b)).astype(BF16))
        decay.append(jnp.exp(b_last))

    scores, update = [], []
    for c in chunks:
        scores.append([_dot_nt(q_in[c][p], k_in[c][:, lanes(p)]) for p in pairs])
        update.append([[_dot_tn(v_ref[rows(c), vals(2 * p + hh)], k_out[c][:, lanes(p)])
                        for hh in range(2)] for p in pairs])
    scores = [[jnp.where(causal, s, 0.0).astype(BF16) for s in sc] for sc in scores]

    state = [st_ref[p] for p in pairs]
    seen = []
    for c in chunks:
        seen.append(state)
        state = [state[p] * decay[c][:, lanes(p)]
                 + jnp.where(lane_lo_v, update[c][p][0], update[c][p][1]) for p in pairs]
    for p in pairs:
        st_ref[p] = state[p]

    for c in chunks:
        for p in pairs:
            o_inter = _dot_nt(q_state[c][p], seen[c][p].astype(BF16))
            for hh in range(2):
                vs = vals(2 * p + hh)
                half = slice(hh * C, (hh + 1) * C)
                o = o_inter[half, :] + _dot(scores[c][p][half, :], v_ref[rows(c), vs])
                finish(o, r_ref[rows(c), vs], rows(c), vs)

    total = functools.reduce(jnp.minimum, [cum[c][C - 1:C, :] for c in chunks])
    risky = jnp.logical_not(jnp.min(total) >= -GLA_SAFE_RANGE)

    @pl.when(risky)
    def _():
        G = 16
        sub = lax.broadcasted_iota(jnp.int32, (G, LANES), 0)
        lane_lo_t = lax.broadcasted_iota(jnp.int32, (G, LANES), 1) < GLA_DK
        raw_ref[...] = jnp.zeros_like(raw_ref)

        def token(t, st):
            grp = pl.ds(pl.multiple_of(lax.shift_left(lax.shift_right_logical(t, 4), 4), G), G)
            here = sub == jnp.bitwise_and(t, G - 1)
            pick = lambda x: jnp.where(here, x, jnp.zeros_like(x))
            new = []
            for p in pairs:
                a = jnp.exp(jnp.sum(pick(la_ref[grp, lanes(p)]), axis=0, keepdims=True))
                k16 = pick(k_ref[grp, lanes(p)]).astype(BF16)
                upd = [_dot_tn(pick(v_ref[grp, vals(2 * p + hh)]), k16)
                       for hh in range(2)]
                sp = st[p] * a + jnp.where(lane_lo_v, upd[0], upd[1])
                o = _dot_nt(_pair_blockdiag(pick(q_ref[grp, lanes(p)]), lane_lo_t),
                            sp.astype(BF16))
                raw_ref[grp, vals(2 * p)] += o[:G, :]
                raw_ref[grp, vals(2 * p + 1)] += o[G:, :]
                new.append(sp)
            return tuple(new)

        st = lax.fori_loop(0, GLA_T, token, tuple(st0_ref[p] for p in pairs))
        for p in pairs:
            st_ref[p] = st[p]
        for c in chunks:
            for h in range(GLA_HEADS):
                finish(raw_ref[rows(c), vals(h)], r_ref[rows(c), vals(h)], rows(c), vals(h))


def _chunk_prefix_matrix():
    i = np.arange(GLA_CHUNK)
    return jnp.asarray(i[None, :] <= i[:, None], dtype=BF16)


def _gla(gq, gk, la, gv, gr, gain):
    B, S, _ = gq.shape
    T = GLA_T
    tri = _chunk_prefix_matrix()
    tok = lambda n: pl.BlockSpec((None, T, n), lambda b, j: (b, j, 0))
    return pl.pallas_call(
        _gla_kernel,
        out_shape=jax.ShapeDtypeStruct((B, S, GLA_V), BF16),
        grid=(B, S // T),
        in_specs=[tok(GLA_QK), tok(GLA_QK), tok(GLA_QK), tok(GLA_V), tok(GLA_V),
                  _const_spec(gain.shape), _const_spec(tri.shape)],
        out_specs=tok(GLA_V),
        scratch_shapes=[pltpu.VMEM((GLA_HEADS // 2, GLA_DV, LANES), F32),
                        pltpu.VMEM((GLA_HEADS // 2, GLA_DV, LANES), F32),
                        pltpu.VMEM((T, GLA_V), F32)],
        compiler_params=pltpu.CompilerParams(
            dimension_semantics=("parallel", "arbitrary"), vmem_limit_bytes=VMEM_LIMIT),
        name="gla",
    )(gq, gk, la, gv, gr, gain, tri)


def _attn_kernel(q_ref, k_ref, kp_ref, v_ref, vp_ref, bias_ref, y_ref, qf, kf, vf, og, mg, dg):
    SP, BL = ATTN_SPAN, ATTN_BLOCK
    first_span = pl.program_id(1) == 0
    qf[...] = q_ref[...].astype(F32)
    kf[0:SP, :] = kp_ref[...].astype(F32)
    kf[SP:, :] = k_ref[...].astype(F32)
    vf[0:SP, :] = vp_ref[...].astype(F32)
    vf[SP:, :] = v_ref[...].astype(F32)
    lane_lo = lax.broadcasted_iota(jnp.int32, (BL, LANES), 1) < ATTN_HEAD_DIM
    ones = jnp.ones((2 * BL, LANES), BF16)

    tiles = []
    for g, (_, d) in enumerate(DILATED_PAIRS):
        for r in range(d):
            for n in range(SP // (BL * d)):
                qs = r + n * (d * BL)
                ks = qs + (SP - d * BL)
                tiles.append((g, d, qs, ks, n == 0))
    groups = [tiles[i:i + ATTN_UNROLL] for i in range(0, len(tiles), ATTN_UNROLL)]

    def score_stage(group):
        return [_dot_nt(_pair_blockdiag(qf[pl.ds(qs, BL, stride=d), :], lane_lo),
                        kf[pl.ds(ks, 2 * BL, stride=d), :].astype(BF16))
                for _, d, qs, ks, _ in group]

    def softmax_stage(group, scores):
        out = []
        for (g, _, _, _, seq_start), s in zip(group, scores):
            sel = jnp.where(first_span, 0, 1) if seq_start else 1
            s = s + bias_ref[g, sel]
            m = jnp.max(s, axis=-1, keepdims=True)
            out.append((m, jnp.exp2(s - m).astype(BF16)))
        return out

    def value_stage(group, stats):
        accs = [_dot(e, jnp.concatenate(
                    [vf[pl.ds(ks, 2 * BL, stride=d), :].astype(BF16), ones], axis=1))
                for (_, d, _, ks, _), (_, e) in zip(group, stats)]
        for (g, d, qs, _, _), (m, _), acc in zip(group, stats, accs):
            rows = pl.ds(qs, BL, stride=d)
            og[g, rows, :] = jnp.where(lane_lo, acc[:BL, :LANES], acc[BL:, :LANES])
            mg[g, rows, :] = jnp.where(lane_lo, m[:BL], m[BL:])
            dg[g, rows, :] = jnp.where(lane_lo, acc[:BL, LANES:], acc[BL:, LANES:])

    scores = score_stage(groups[0])
    for i, group in enumerate(groups):
        ahead = score_stage(groups[i + 1]) if i + 1 < len(groups) else None
        value_stage(group, softmax_stage(group, scores))
        scores = ahead

    def merge(c, carry):
        rows = pl.ds(pl.multiple_of(c * ATTN_MERGE_ROWS, ATTN_MERGE_ROWS), ATTN_MERGE_ROWS)
        ms = [mg[g, rows, :] for g in range(len(DILATED_PAIRS))]
        top = functools.reduce(jnp.maximum, ms)
        ws = [jnp.exp2(mi - top) for mi in ms]
        num = sum(w * og[g, rows, :] for g, w in enumerate(ws))
        den = sum(w * dg[g, rows, :] for g, w in enumerate(ws))
        y_ref[rows, :] = (num / den).astype(y_ref.dtype)
        return carry

    lax.fori_loop(0, SP // ATTN_MERGE_ROWS, merge, 0)


def _attn_bias():
    span = DILATED_PAIRS[0][0] // DILATED_PAIRS[0][1]
    slopes = np.asarray([2.0 ** (-8.0 * (h + 1) / ATTN_HEADS) for h in range(ATTN_HEADS)],
                        dtype=np.float32)
    iq = np.arange(ATTN_BLOCK)[:, None]
    ik = np.arange(2 * ATTN_BLOCK)[None, :]
    rel = iq + ATTN_BLOCK - ik
    valid = (rel >= 0) & (rel <= span)
    out = np.empty((len(DILATED_PAIRS), 2, ATTN_HEADS, ATTN_BLOCK, 2 * ATTN_BLOCK), np.float32)
    for g, (_, d) in enumerate(DILATED_PAIRS):
        for first in (0, 1):
            ok = valid & ((ik >= ATTN_BLOCK) if first == 0 else True)
            for h in range(ATTN_HEADS):
                alibi = -slopes[h] * (d * rel).astype(np.float32) * np.float32(LOG2E)
                out[g, first, h] = np.where(ok, alibi, np.float32(NEG_BIG))
    return out.reshape(len(DILATED_PAIRS), 2, ATTN_HEADS // 2, 2 * ATTN_BLOCK, 2 * ATTN_BLOCK)


def _attn(aq, ak, av):
    B, S, _ = aq.shape
    SP = ATTN_SPAN
    G = len(DILATED_PAIRS)
    bias = jnp.asarray(_attn_bias())
    cur = pl.BlockSpec((None, SP, LANES), lambda b, j, p: (b, j, p))
    prev = pl.BlockSpec((None, SP, LANES), lambda b, j, p: (b, jnp.maximum(j - 1, 0), p))
    return pl.pallas_call(
        _attn_kernel,
        out_shape=jax.ShapeDtypeStruct((B, S, ATTN_DIM), BF16),
        grid=(B, S // SP, ATTN_HEADS // 2),
        in_specs=[cur, cur, prev, cur, prev,
                  pl.BlockSpec((G, 2, None, 2 * ATTN_BLOCK, 2 * ATTN_BLOCK),
                               lambda b, j, p: (0, 0, p, 0, 0))],
        out_specs=cur,
        scratch_shapes=[pltpu.VMEM((SP, LANES), F32), pltpu.VMEM((2 * SP, LANES), F32),
                        pltpu.VMEM((2 * SP, LANES), F32), pltpu.VMEM((G, SP, LANES), F32),
                        pltpu.VMEM((G, SP, LANES), F32), pltpu.VMEM((G, SP, LANES), F32)],
        compiler_params=pltpu.CompilerParams(
            dimension_semantics=("parallel", "parallel", "parallel"),
            vmem_limit_bytes=VMEM_LIMIT),
        name="attn",
    )(aq, ak, ak, av, av, bias)


def _ffn_kernel(x_ref, yg_ref, ya_ref, mod_ref, wout_ref, g2_ref, wup_ref, cw_ref, cb_ref,
                wdn_ref, out_ref, carry_ref, act_ref):
    ts = FFN_SUB
    subs = [slice(i * ts, (i + 1) * ts) for i in range(FFN_TM // ts)]

    @pl.when(pl.program_id(1) == 0)
    def _():
        carry_ref[...] = jnp.zeros_like(carry_ref)

    rid = lax.broadcasted_iota(jnp.int32, (8, FFN_CH), 0)

    def conv(u, c0):
        cols = slice(c0, c0 + FFN_CH)
        c6 = carry_ref[6:7, cols]
        c7 = carry_ref[7:8, cols]
        r1 = pltpu.roll(u, 1, 0)
        r2 = pltpu.roll(u, 2, 0)
        u1 = jnp.concatenate([jnp.where(rid == 0, c7, r1[:8])] + [r1[8:]], axis=0)
        u2 = jnp.concatenate(
            [jnp.where(rid == 0, c6, jnp.where(rid == 1, c7, r2[:8]))] + [r2[8:]], axis=0)
        carry_ref[:, cols] = u[ts - 8:, :]
        return (cb_ref[:, cols] + cw_ref[0:1, cols] * u2 + cw_ref[1:2, cols] * u1
                + cw_ref[2:3, cols] * u)

    attn_out = [_dot(jnp.concatenate([yg_ref[rs, :], ya_ref[rs, :]], axis=-1), wout_ref[...])
                for rs in subs]
    hb = []
    for rs, a in zip(subs, attn_out):
        x1 = x_ref[rs, :] + mod_ref[2:3, :] * a
        out_ref[rs, :] = x1
        ms = jnp.mean(x1 * x1, axis=-1, keepdims=True)
        h = x1 * lax.rsqrt(ms + EPS) * g2_ref[...]
        hb.append((h * (1.0 + mod_ref[4:5, :]) + mod_ref[3:4, :]).astype(BF16))
    for rs, hs in zip(subs, hb):
        for c in range(D_FF // FFN_CH):
            g0 = c * FFN_CH
            v0 = D_FF + c * FFN_CH
            ug = conv(_dot(hs, wup_ref[:, g0:g0 + FFN_CH]), g0)
            uv = conv(_dot(hs, wup_ref[:, v0:v0 + FFN_CH]), v0)
            act_ref[rs, g0:g0 + FFN_CH] = ((ug / (1.0 + jnp.exp(-ug))) * uv).astype(BF16)
    for rs in subs:
        out_ref[rs, :] = out_ref[rs, :] + mod_ref[5:6, :] * _dot(act_ref[rs, :], wdn_ref[...])


def _ffn(x, yg, ya, mod3, w_out, g2, w_up, conv_w, conv_b, w_dn):
    B, S, _ = x.shape
    tm = FFN_TM
    tok = lambda n: pl.BlockSpec((None, tm, n), lambda b, i: (b, i, 0))
    return pl.pallas_call(
        _ffn_kernel,
        out_shape=jax.ShapeDtypeStruct((B, S, D_MODEL), F32),
        grid=(B, S // tm),
        in_specs=[tok(D_MODEL), tok(GLA_V), tok(ATTN_DIM),
                  pl.BlockSpec((None, N_MOD, D_MODEL), lambda b, i: (b, 0, 0)),
                  _const_spec(w_out.shape), _const_spec(g2.shape), _const_spec(w_up.shape),
                  _const_spec(conv_w.shape), _const_spec(conv_b.shape), _const_spec(w_dn.shape)],
        out_specs=tok(D_MODEL),
        scratch_shapes=[pltpu.VMEM((8, 2 * D_FF), F32), pltpu.VMEM((tm, D_FF), BF16)],
        compiler_params=pltpu.CompilerParams(
            dimension_semantics=("parallel", "arbitrary"), vmem_limit_bytes=VMEM_LIMIT),
        name="ffn",
    )(x, yg, ya, mod3, w_out, g2, w_up, conv_w, conv_b, w_dn)


def _head_ones():
    heads = MXU_DEPTH // ATTN_HEAD_DIM
    blk = np.kron(np.eye(heads, dtype=np.float32),
                  np.ones((ATTN_HEAD_DIM, ATTN_HEAD_DIM), np.float32))
    return jnp.asarray(blk, dtype=BF16)


def kernel(x, c, w_ada, b_ada, norm1_g, w_in, gla_w_gate, gla_b_gate, gla_norm_g, q_norm_g,
           k_norm_g, w_out, norm2_g, w_up, conv_w, conv_b, w_down):
    B = x.shape[0]
    depth = w_ada.shape[0]
    ones_blk = _head_ones()
    glr0 = 2 * GLA_QK + 2 * GLA_V
    for l in range(depth):
        mod3 = _ada(c, w_ada[l], b_ada[l]).reshape(B, N_MOD, D_MODEL)
        w = w_in[l]
        w_main = jnp.concatenate([w[:, :glr0], w[:, glr0 + GLA_GATE_RANK:]], axis=1).astype(BF16)
        w_glr = jnp.pad(w[:, glr0:glr0 + GLA_GATE_RANK],
                        ((0, 0), (0, LANES - GLA_GATE_RANK))).astype(BF16)
        w_gate = jnp.pad(gla_w_gate[l], ((0, LANES - GLA_GATE_RANK), (0, 0))).astype(BF16)
        gq, gk, la, gv, gr, aq, ak, av = _inproj(
            x, mod3, norm1_g[l].reshape(1, -1), w_main, w_glr, w_gate,
            gla_b_gate[l].reshape(1, -1), ones_blk,
            jnp.tile(q_norm_g[l], ATTN_HEADS).reshape(1, -1),
            jnp.tile(k_norm_g[l], ATTN_HEADS).reshape(1, -1))
        yg = _gla(gq, gk, la, gv, gr, gla_norm_g[l].reshape(1, -1))
        ya = _attn(aq, ak, av)
        x = _ffn(x, yg, ya, mod3, w_out[l].astype(BF16), norm2_g[l].reshape(1, -1),
                 w_up[l].astype(BF16), conv_w[l], conv_b[l].reshape(1, -1),
                 w_down[l].astype(BF16))
    return x
```

```python
import functools

import jax
import jax.numpy as jnp
import numpy as np
from jax import lax
from jax.experimental import pallas as pl
from jax.experimental.pallas import tpu as pltpu

F32 = jnp.float32
BF16 = jnp.bfloat16

D_MODEL = 1024
GLA_HEADS = 4
GLA_DK = 64
GLA_DV = 128
GLA_GATE_RANK = 16
GLA_GATE_TAU = 16.0
GLA_CHUNK = 64
GLA_QK = GLA_HEADS * GLA_DK
GLA_V = GLA_HEADS * GLA_DV
ATTN_HEADS = 8
ATTN_HEAD_DIM = 64
ATTN_DIM = ATTN_HEADS * ATTN_HEAD_DIM
DILATED_PAIRS = ((128, 1), (512, 4), (2048, 16))
ATTN_BLOCK = 128
D_FF = 2816
CONV_WIDTH = 3
N_MOD = 6
EPS = 1e-6

LANES = 128
MXU_DEPTH = 256
NEG_BIG = -1e30
LOG2E = 1.4426950408889634

INPROJ_TM = 512
GLA_T = 512
GLA_SAFE_RANGE = 60.0
ATTN_SPAN = 2048
ATTN_UNROLL = 4
ATTN_MERGE_ROWS = 64
FFN_TM = 1024
FFN_SUB = 512
FFN_CH = 256
VMEM_LIMIT = 56 * 1024 * 1024


def _const_spec(shape):
    nd = len(shape)
    return pl.BlockSpec(shape, lambda *_: (0,) * nd, pipeline_mode=pl.Buffered(1))


def _dot(a, b):
    return jnp.dot(a, b, preferred_element_type=F32)


def _dot_nt(a, b):
    return lax.dot_general(a, b, (((1,), (1,)), ((), ())), preferred_element_type=F32)


def _dot_tn(a, b):
    return lax.dot_general(a, b, (((0,), (0,)), ((), ())), preferred_element_type=F32)


def _split_bf16(x):
    hi = x.astype(BF16)
    lo = (x - hi.astype(F32)).astype(BF16)
    return hi, lo


def _ada_kernel(c_ref, w_ref, b_ref, o_ref):
    c = c_ref[...]
    cond = c / (1.0 + jnp.exp(-c))
    o_ref[...] = _dot(cond.astype(BF16), w_ref[...].astype(BF16)) + b_ref[...]


def _ada(c, w_ada, b_ada):
    B = c.shape[0]
    n = w_ada.shape[1]
    bn = D_MODEL
    return pl.pallas_call(
        _ada_kernel,
        out_shape=jax.ShapeDtypeStruct((B, n), F32),
        grid=(n // bn,),
        in_specs=[
            pl.BlockSpec((B, D_MODEL), lambda j: (0, 0)),
            pl.BlockSpec((D_MODEL, bn), lambda j: (0, j)),
            pl.BlockSpec((1, bn), lambda j: (0, j)),
        ],
        out_specs=pl.BlockSpec((B, bn), lambda j: (0, j)),
        name="ada",
    )(c, w_ada, b_ada.reshape(1, n))


def _inproj_kernel(x_ref, mod_ref, g1_ref, w_ref, wglr_ref, wgate_ref, bgate_ref, ones_ref,
                   qg_ref, kg_ref,
                   gq_ref, gk_ref, la_ref, gv_ref, gr_ref, aq_ref, ak_ref, av_ref):
    x = x_ref[...]
    ms = jnp.mean(x * x, axis=-1, keepdims=True)
    y = x * lax.rsqrt(ms + EPS) * g1_ref[...]
    h = y * (1.0 + mod_ref[1:2, :]) + mod_ref[0:1, :]
    hb = h.astype(BF16)

    def proj(c0, n):
        return _dot(hb, w_ref[:, c0:c0 + n])

    def head_norm(p, gain):
        gw = ones_ref.shape[0]
        hi, lo = _split_bf16(p * p)
        ssum = jnp.concatenate(
            [_dot(hi[:, c:c + gw], ones_ref[...]) + _dot(lo[:, c:c + gw], ones_ref[...])
             for c in range(0, ATTN_DIM, gw)], axis=-1)
        return p * lax.rsqrt(ssum * (1.0 / ATTN_HEAD_DIM) + EPS) * gain

    a0 = 2 * GLA_QK + 2 * GLA_V
    glr = _dot(hb, wglr_ref[...])
    aq_raw = proj(a0, ATTN_DIM)
    ak_raw = proj(a0 + ATTN_DIM, ATTN_DIM)
    gq_ref[...] = (proj(0, GLA_QK) * (GLA_DK ** -0.5)).astype(gq_ref.dtype)
    gk_ref[...] = proj(GLA_QK, GLA_QK).astype(gk_ref.dtype)
    gv_ref[...] = proj(2 * GLA_QK, GLA_V).astype(BF16)
    gr_ref[...] = proj(2 * GLA_QK + GLA_V, GLA_V).astype(gr_ref.dtype)
    av_ref[...] = proj(a0 + 2 * ATTN_DIM, ATTN_DIM).astype(BF16)

    z = _dot(glr.astype(BF16), wgate_ref[...]) + bgate_ref[...]
    log_sig = jnp.minimum(z, 0.0) - jnp.log1p(jnp.exp(-jnp.abs(z)))
    la_ref[...] = log_sig * (1.0 / GLA_GATE_TAU)
    aq_ref[...] = (head_norm(aq_raw, qg_ref[...]) * (ATTN_HEAD_DIM ** -0.5 * LOG2E)).astype(BF16)
    ak_ref[...] = head_norm(ak_raw, kg_ref[...]).astype(BF16)


def _inproj(x, mod3, g1, w_main, w_glr, w_gate, b_gate, ones_blk, qg, kg):
    B, S, _ = x.shape
    tm = INPROJ_TM
    tok = lambda n: pl.BlockSpec((None, tm, n), lambda b, i: (b, i, 0))
    out_dims = [(GLA_QK, BF16), (GLA_QK, BF16), (GLA_QK, F32), (GLA_V, BF16), (GLA_V, BF16),
                (ATTN_DIM, BF16), (ATTN_DIM, BF16), (ATTN_DIM, BF16)]
    return pl.pallas_call(
        _inproj_kernel,
        out_shape=[jax.ShapeDtypeStruct((B, S, n), dt) for n, dt in out_dims],
        grid=(B, S // tm),
        in_specs=[
            tok(D_MODEL),
            pl.BlockSpec((None, N_MOD, D_MODEL), lambda b, i: (b, 0, 0)),
            _const_spec(g1.shape), _const_spec(w_main.shape), _const_spec(w_glr.shape),
            _const_spec(w_gate.shape), _const_spec(b_gate.shape), _const_spec(ones_blk.shape),
            _const_spec(qg.shape), _const_spec(kg.shape),
        ],
        out_specs=[tok(n) for n, _ in out_dims],
        compiler_params=pltpu.CompilerParams(
            dimension_semantics=("parallel", "parallel"), vmem_limit_bytes=VMEM_LIMIT),
        name="inproj",
    )(x, mod3, g1, w_main, w_glr, w_gate, b_gate, ones_blk, qg, kg)


def _pair_blockdiag(a, lane_lo):
    zero = jnp.zeros_like(a)
    return jnp.concatenate([jnp.where(lane_lo, a, zero), jnp.where(lane_lo, zero, a)],
                           axis=0).astype(BF16)


def _gla_kernel(q_ref, k_ref, la_ref, v_ref, r_ref, g_ref, tri_ref, o_ref,
                st_ref, st0_ref, raw_ref):
    C = GLA_CHUNK
    chunks = range(GLA_T // C)
    pairs = range(GLA_HEADS // 2)
    lanes = lambda p: slice(p * LANES, (p + 1) * LANES)
    rows = lambda c: slice(c * C, (c + 1) * C)
    vals = lambda h: slice(h * GLA_DV, (h + 1) * GLA_DV)

    @pl.when(pl.program_id(1) == 0)
    def _():
        st_ref[...] = jnp.zeros_like(st_ref)

    st0_ref[...] = st_ref[...]

    def finish(o, r, rsel, vs):
        ms = jnp.mean(o * o, axis=-1, keepdims=True)
        o = o * lax.rsqrt(ms + EPS) * g_ref[...]
        r = r.astype(F32)
        o_ref[rsel, vs] = (o * (r / (1.0 + jnp.exp(-r)))).astype(o_ref.dtype)

    row2 = lax.broadcasted_iota(jnp.int32, (2 * C, C), 0)
    col2 = lax.broadcasted_iota(jnp.int32, (2 * C, C), 1)
    causal = col2 <= jnp.where(row2 >= C, row2 - C, row2)
    lane_lo = lax.broadcasted_iota(jnp.int32, (C, LANES), 1) < GLA_DK
    lane_lo_v = lax.broadcasted_iota(jnp.int32, (GLA_DV, LANES), 1) < GLA_DK

    hi, lo = _split_bf16(la_ref[...])
    cum = [_dot(tri_ref[...], hi[rows(c), :]) + _dot(tri_ref[...], lo[rows(c), :])
           for c in chunks]

    q_state, q_in, k_in, k_out, decay = [], [], [], [], []
    for c in chunks:
        b = cum[c]
        b_last = b[C - 1:C, :]
        b_mid = b[C // 2:C // 2 + 1, :]
        q = q_ref[rows(c), :].astype(F32)
        k = k_ref[rows(c), :].astype(F32)
        qs = q * jnp.exp(b)
        qi = q * jnp.exp(b - b_mid)
        q_state.append([_pair_blockdiag(qs[:, lanes(p)], lane_lo) for p in pairs])
        q_in.append([_pair_blockdiag(qi[:, lanes(p)], lane_lo) for p in pairs])
        k_in.append((k * jnp.exp(b_mid - b)).astype(BF16))
        k_out.append((k * jnp.exp(b_last - b)).astype(BF16))
        decay.append(jnp.exp(b_last))

    scores, update = [], []
    for c in chunks:
        scores.append([_dot_nt(q_in[c][p], k_in[c][:, lanes(p)]) for p in pairs])
        update.append([[_dot_tn(v_ref[rows(c), vals(2 * p + hh)], k_out[c][:, lanes(p)])
                        for hh in range(2)] for p in pairs])
    scores = [[jnp.where(causal, s, 0.0).astype(BF16) for s in sc] for sc in scores]

    state = [st_ref[p] for p in pairs]
    seen = []
    for c in chunks:
        seen.append(state)
        state = [state[p] * decay[c][:, lanes(p)]
                 + jnp.where(lane_lo_v, update[c][p][0], update[c][p][1]) for p in pairs]
    for p in pairs:
        st_ref[p] = state[p]

    for c in chunks:
        for p in pairs:
            o_inter = _dot_nt(q_state[c][p], seen[c][p].astype(BF16))
            for hh in range(2):
                vs = vals(2 * p + hh)
                half = slice(hh * C, (hh + 1) * C)
                o = o_inter[half, :] + _dot(scores[c][p][half, :], v_ref[rows(c), vs])
                finish(o, r_ref[rows(c), vs], rows(c), vs)

    total = functools.reduce(jnp.minimum, [cum[c][C - 1:C, :] for c in chunks])
    risky = jnp.logical_not(jnp.min(total) >= -GLA_SAFE_RANGE)

    @pl.when(risky)
    def _():
        G = 16
        sub = lax.broadcasted_iota(jnp.int32, (G, LANES), 0)
        lane_lo_t = lax.broadcasted_iota(jnp.int32, (G, LANES), 1) < GLA_DK
        raw_ref[...] = jnp.zeros_like(raw_ref)

        def token(t, st):
            grp = pl.ds(pl.multiple_of(lax.shift_left(lax.shift_right_logical(t, 4), 4), G), G)
            here = sub == jnp.bitwise_and(t, G - 1)
            pick = lambda x: jnp.where(here, x, jnp.zeros_like(x))
            new = []
            for p in pairs:
                a = jnp.exp(jnp.sum(pick(la_ref[grp, lanes(p)]), axis=0, keepdims=True))
                k16 = pick(k_ref[grp, lanes(p)]).astype(BF16)
                upd = [_dot_tn(pick(v_ref[grp, vals(2 * p + hh)]), k16)
                       for hh in range(2)]
                sp = st[p] * a + jnp.where(lane_lo_v, upd[0], upd[1])
                o = _dot_nt(_pair_blockdiag(pick(q_ref[grp, lanes(p)]), lane_lo_t),
                            sp.astype(BF16))
                raw_ref[grp, vals(2 * p)] += o[:G, :]
                raw_ref[grp, vals(2 * p + 1)] += o[G:, :]
                new.append(sp)
            return tuple(new)

        st = lax.fori_loop(0, GLA_T, token, tuple(st0_ref[p] for p in pairs))
        for p in pairs:
            st_ref[p] = st[p]
        for c in chunks:
            for h in range(GLA_HEADS):
                finish(raw_ref[rows(c), vals(h)], r_ref[rows(c), vals(h)], rows(c), vals(h))


def _chunk_prefix_matrix():
    i = np.arange(GLA_CHUNK)
    return jnp.asarray(i[None, :] <= i[:, None], dtype=BF16)


def _gla(gq, gk, la, gv, gr, gain):
    B, S, _ = gq.shape
    T = GLA_T
    tri = _chunk_prefix_matrix()
    tok = lambda n: pl.BlockSpec((None, T, n), lambda b, j: (b, j, 0))
    return pl.pallas_call(
        _gla_kernel,
        out_shape=jax.ShapeDtypeStruct((B, S, GLA_V), BF16),
        grid=(B, S // T),
        in_specs=[tok(GLA_QK), tok(GLA_QK), tok(GLA_QK), tok(GLA_V), tok(GLA_V),
                  _const_spec(gain.shape), _const_spec(tri.shape)],
        out_specs=tok(GLA_V),
        scratch_shapes=[pltpu.VMEM((GLA_HEADS // 2, GLA_DV, LANES), F32),
                        pltpu.VMEM((GLA_HEADS // 2, GLA_DV, LANES), F32),
                        pltpu.VMEM((T, GLA_V), F32)],
        compiler_params=pltpu.CompilerParams(
            dimension_semantics=("parallel", "arbitrary"), vmem_limit_bytes=VMEM_LIMIT),
        name="gla",
    )(gq, gk, la, gv, gr, gain, tri)


def _attn_kernel(q_ref, k_ref, kp_ref, v_ref, vp_ref, bias_ref, y_ref, qf, kf, vf, og, mg, dg):
    SP, BL = ATTN_SPAN, ATTN_BLOCK
    first_span = pl.program_id(1) == 0
    qf[...] = q_ref[...].astype(F32)
    kf[0:SP, :] = kp_ref[...].astype(F32)
    kf[SP:, :] = k_ref[...].astype(F32)
    vf[0:SP, :] = vp_ref[...].astype(F32)
    vf[SP:, :] = v_ref[...].astype(F32)
    lane_lo = lax.broadcasted_iota(jnp.int32, (BL, LANES), 1) < ATTN_HEAD_DIM
    ones = jnp.ones((2 * BL, LANES), BF16)

    per_dilation = []
    for g, (_, d) in enumerate(DILATED_PAIRS):
        blocks = []
        for r in range(d):
            for n in range(SP // (BL * d)):
                qs = r + n * (d * BL)
                ks = qs + (SP - d * BL)
                blocks.append((g, d, qs, ks, n == 0))
        per_dilation.append(blocks)
    tiles = [t for same_index in zip(*per_dilation) for t in same_index]
    groups = [tiles[i:i + ATTN_UNROLL] for i in range(0, len(tiles), ATTN_UNROLL)]

    def score_stage(group):
        return [_dot_nt(_pair_blockdiag(qf[pl.ds(qs, BL, stride=d), :], lane_lo),
                        kf[pl.ds(ks, 2 * BL, stride=d), :].astype(BF16))
                for _, d, qs, ks, _ in group]

    def softmax_stage(group, scores):
        out = []
        for (g, _, _, _, seq_start), s in zip(group, scores):
            sel = jnp.where(first_span, 0, 1) if seq_start else 1
            s = s + bias_ref[g, sel]
            m = jnp.max(s, axis=-1, keepdims=True)
            out.append((m, jnp.exp2(s - m).astype(BF16)))
        return out

    def value_stage(group, stats):
        accs = [_dot(e, jnp.concatenate(
                    [vf[pl.ds(ks, 2 * BL, stride=d), :].astype(BF16), ones], axis=1))
                for (_, d, _, ks, _), (_, e) in zip(group, stats)]
        for (g, d, qs, _, _), (m, _), acc in zip(group, stats, accs):
            rows = pl.ds(qs, BL, stride=d)
            og[g, rows, :] = jnp.where(lane_lo, acc[:BL, :LANES], acc[BL:, :LANES])
            mg[g, rows, :] = jnp.where(lane_lo, m[:BL], m[BL:])
            dg[g, rows, :] = jnp.where(lane_lo, acc[:BL, LANES:], acc[BL:, LANES:])

    scores = score_stage(groups[0])
    for i, group in enumerate(groups):
        ahead = score_stage(groups[i + 1]) if i + 1 < len(groups) else None
        value_stage(group, softmax_stage(group, scores))
        scores = ahead

    def merge(c, carry):
        rows = pl.ds(pl.multiple_of(c * ATTN_MERGE_ROWS, ATTN_MERGE_ROWS), ATTN_MERGE_ROWS)
        ms = [mg[g, rows, :] for g in range(len(DILATED_PAIRS))]
        top = functools.reduce(jnp.maximum, ms)
        ws = [jnp.exp2(mi - top) for mi in ms]
        num = sum(w * og[g, rows, :] for g, w in enumerate(ws))
        den = sum(w * dg[g, rows, :] for g, w in enumerate(ws))
        y_ref[rows, :] = (num / den).astype(y_ref.dtype)
        return carry

    lax.fori_loop(0, SP // ATTN_MERGE_ROWS, merge, 0, unroll=4)


def _attn_bias():
    span = DILATED_PAIRS[0][0] // DILATED_PAIRS[0][1]
    slopes = np.asarray([2.0 ** (-8.0 * (h + 1) / ATTN_HEADS) for h in range(ATTN_HEADS)],
                        dtype=np.float32)
    iq = np.arange(ATTN_BLOCK)[:, None]
    ik = np.arange(2 * ATTN_BLOCK)[None, :]
    rel = iq + ATTN_BLOCK - ik
    valid = (rel >= 0) & (rel <= span)
    out = np.empty((len(DILATED_PAIRS), 2, ATTN_HEADS, ATTN_BLOCK, 2 * ATTN_BLOCK), np.float32)
    for g, (_, d) in enumerate(DILATED_PAIRS):
        for first in (0, 1):
            ok = valid & ((ik >= ATTN_BLOCK) if first == 0 else True)
            for h in range(ATTN_HEADS):
                alibi = -slopes[h] * (d * rel).astype(np.float32) * np.float32(LOG2E)
                out[g, first, h] = np.where(ok, alibi, np.float32(NEG_BIG))
    return out.reshape(len(DILATED_PAIRS), 2, ATTN_HEADS // 2, 2 * ATTN_BLOCK, 2 * ATTN_BLOCK)


def _attn(aq, ak, av):
    B, S, _ = aq.shape
    SP = ATTN_SPAN
    G = len(DILATED_PAIRS)
    bias = jnp.asarray(_attn_bias())
    cur = pl.BlockSpec((None, SP, LANES), lambda b, j, p: (b, j, p))
    prev = pl.BlockSpec((None, SP, LANES), lambda b, j, p: (b, jnp.maximum(j - 1, 0), p))
    return pl.pallas_call(
        _attn_kernel,
        out_shape=jax.ShapeDtypeStruct((B, S, ATTN_DIM), BF16),
        grid=(B, S // SP, ATTN_HEADS // 2),
        in_specs=[cur, cur, prev, cur, prev,
                  pl.BlockSpec((G, 2, None, 2 * ATTN_BLOCK, 2 * ATTN_BLOCK),
                               lambda b, j, p: (0, 0, p, 0, 0))],
        out_specs=cur,
        scratch_shapes=[pltpu.VMEM((SP, LANES), F32), pltpu.VMEM((2 * SP, LANES), F32),
                        pltpu.VMEM((2 * SP, LANES), F32), pltpu.VMEM((G, SP, LANES), F32),
                        pltpu.VMEM((G, SP, LANES), F32), pltpu.VMEM((G, SP, LANES), F32)],
        compiler_params=pltpu.CompilerParams(
            dimension_semantics=("parallel", "parallel", "parallel"),
            vmem_limit_bytes=VMEM_LIMIT),
        name="attn",
    )(aq, ak, ak, av, av, bias)


def _ffn_kernel(x_ref, yg_ref, ya_ref, mod_ref, wout_ref, g2_ref, wup_ref, cw_ref, cb_ref,
                wdn_ref, out_ref, carry_ref, act_ref):
    ts = FFN_SUB
    subs = [slice(i * ts, (i + 1) * ts) for i in range(FFN_TM // ts)]

    @pl.when(pl.program_id(1) == 0)
    def _():
        carry_ref[...] = jnp.zeros_like(carry_ref)

    rid = lax.broadcasted_iota(jnp.int32, (8, FFN_CH), 0)

    def conv(u, c0):
        cols = slice(c0, c0 + FFN_CH)
        c6 = carry_ref[6:7, cols]
        c7 = carry_ref[7:8, cols]
        r1 = pltpu.roll(u, 1, 0)
        r2 = pltpu.roll(u, 2, 0)
        u1 = jnp.concatenate([jnp.where(rid == 0, c7, r1[:8])] + [r1[8:]], axis=0)
        u2 = jnp.concatenate(
            [jnp.where(rid == 0, c6, jnp.where(rid == 1, c7, r2[:8]))] + [r2[8:]], axis=0)
        carry_ref[:, cols] = u[ts - 8:, :]
        return (cb_ref[:, cols] + cw_ref[0:1, cols] * u2 + cw_ref[1:2, cols] * u1
                + cw_ref[2:3, cols] * u)

    attn_out = [_dot(jnp.concatenate([yg_ref[rs, :], ya_ref[rs, :]], axis=-1), wout_ref[...])
                for rs in subs]
    hb = []
    for rs, a in zip(subs, attn_out):
        x1 = x_ref[rs, :] + mod_ref[2:3, :] * a
        out_ref[rs, :] = x1
        ms = jnp.mean(x1 * x1, axis=-1, keepdims=True)
        h = x1 * lax.rsqrt(ms + EPS) * g2_ref[...]
        hb.append((h * (1.0 + mod_ref[4:5, :]) + mod_ref[3:4, :]).astype(BF16))
    for rs, hs in zip(subs, hb):
        for c in range(D_FF // FFN_CH):
            g0 = c * FFN_CH
            v0 = D_FF + c * FFN_CH
            ug = conv(_dot(hs, wup_ref[:, g0:g0 + FFN_CH]), g0)
            uv = conv(_dot(hs, wup_ref[:, v0:v0 + FFN_CH]), v0)
            act_ref[rs, g0:g0 + FFN_CH] = ((ug / (1.0 + jnp.exp(-ug))) * uv).astype(BF16)
    for rs in subs:
        out_ref[rs, :] = out_ref[rs, :] + mod_ref[5:6, :] * _dot(act_ref[rs, :], wdn_ref[...])


def _ffn(x, yg, ya, mod3, w_out, g2, w_up, conv_w, conv_b, w_dn):
    B, S, _ = x.shape
    tm = FFN_TM
    tok = lambda n: pl.BlockSpec((None, tm, n), lambda b, i: (b, i, 0))
    return pl.pallas_call(
        _ffn_kernel,
        out_shape=jax.ShapeDtypeStruct((B, S, D_MODEL), F32),
        grid=(B, S // tm),
        in_specs=[tok(D_MODEL), tok(GLA_V), tok(ATTN_DIM),
                  pl.BlockSpec((None, N_MOD, D_MODEL), lambda b, i: (b, 0, 0)),
                  _const_spec(w_out.shape), _const_spec(g2.shape), _const_spec(w_up.shape),
                  _const_spec(conv_w.shape), _const_spec(conv_b.shape), _const_spec(w_dn.shape)],
        out_specs=tok(D_MODEL),
        scratch_shapes=[pltpu.VMEM((8, 2 * D_FF), F32), pltpu.VMEM((tm, D_FF), BF16)],
        compiler_params=pltpu.CompilerParams(
            dimension_semantics=("parallel", "arbitrary"), vmem_limit_bytes=VMEM_LIMIT),
        name="ffn",
    )(x, yg, ya, mod3, w_out, g2, w_up, conv_w, conv_b, w_dn)


def _head_ones():
    heads = MXU_DEPTH // ATTN_HEAD_DIM
    blk = np.kron(np.eye(heads, dtype=np.float32),
                  np.ones((ATTN_HEAD_DIM, ATTN_HEAD_DIM), np.float32))
    return jnp.asarray(blk, dtype=BF16)


def kernel(x, c, w_ada, b_ada, norm1_g, w_in, gla_w_gate, gla_b_gate, gla_norm_g, q_norm_g,
           k_norm_g, w_out, norm2_g, w_up, conv_w, conv_b, w_down):
    B = x.shape[0]
    depth = w_ada.shape[0]
    ones_blk = _head_ones()
    glr0 = 2 * GLA_QK + 2 * GLA_V
    for l in range(depth):
        mod3 = _ada(c, w_ada[l], b_ada[l]).reshape(B, N_MOD, D_MODEL)
        w = w_in[l]
        w_main = jnp.concatenate([w[:, :glr0], w[:, glr0 + GLA_GATE_RANK:]], axis=1).astype(BF16)
        w_glr = jnp.pad(w[:, glr0:glr0 + GLA_GATE_RANK],
                        ((0, 0), (0, LANES - GLA_GATE_RANK))).astype(BF16)
        w_gate = jnp.pad(gla_w_gate[l], ((0, LANES - GLA_GATE_RANK), (0, 0))).astype(BF16)
        gq, gk, la, gv, gr, aq, ak, av = _inproj(
            x, mod3, norm1_g[l].reshape(1, -1), w_main, w_glr, w_gate,
            gla_b_gate[l].reshape(1, -1), ones_blk,
            jnp.tile(q_norm_g[l], ATTN_HEADS).reshape(1, -1),
            jnp.tile(k_norm_g[l], ATTN_HEADS).reshape(1, -1))
        yg = _gla(gq, gk, la, gv, gr, gla_norm_g[l].reshape(1, -1))
        ya = _attn(aq, ak, av)
        x = _ffn(x, yg, ya, mod3, w_out[l].astype(BF16), norm2_g[l].reshape(1, -1),
                 w_up[l].astype(BF16), conv_w[l], conv_b[l].reshape(1, -1),
                 w_down[l].astype(BF16))
    return x
```

```python
import functools

import jax
import jax.numpy as jnp
import numpy as np
from jax import lax
from jax.experimental import pallas as pl
from jax.experimental.pallas import tpu as pltpu

F32 = jnp.float32
BF16 = jnp.bfloat16

D_MODEL = 1024
GLA_HEADS = 4
GLA_DK = 64
GLA_DV = 128
GLA_GATE_RANK = 16
GLA_GATE_TAU = 16.0
GLA_CHUNK = 64
GLA_QK = GLA_HEADS * GLA_DK
GLA_V = GLA_HEADS * GLA_DV
ATTN_HEADS = 8
ATTN_HEAD_DIM = 64
ATTN_DIM = ATTN_HEADS * ATTN_HEAD_DIM
DILATED_PAIRS = ((128, 1), (512, 4), (2048, 16))
ATTN_BLOCK = 128
D_FF = 2816
CONV_WIDTH = 3
N_MOD = 6
EPS = 1e-6

LANES = 128
MXU_DEPTH = 256
NEG_BIG = -1e30
LOG2E = 1.4426950408889634

INPROJ_TM = 1024
INPROJ_SUB = 512
GLA_T = 512
GLA_SAFE_RANGE = 60.0
ATTN_SPAN = 2048
ATTN_UNROLL = 4
ATTN_MERGE_ROWS = 64
FFN_TM = 1024
FFN_SUB = 512
FFN_CH = 256
VMEM_LIMIT = 56 * 1024 * 1024


def _const_spec(shape):
    nd = len(shape)
    return pl.BlockSpec(shape, lambda *_: (0,) * nd, pipeline_mode=pl.Buffered(1))


def _dot(a, b):
    return jnp.dot(a, b, preferred_element_type=F32)


def _dot_nt(a, b):
    return lax.dot_general(a, b, (((1,), (1,)), ((), ())), preferred_element_type=F32)


def _dot_tn(a, b):
    return lax.dot_general(a, b, (((0,), (0,)), ((), ())), preferred_element_type=F32)


def _split_bf16(x):
    hi = x.astype(BF16)
    lo = (x - hi.astype(F32)).astype(BF16)
    return hi, lo


def _ada_kernel(c_ref, w_ref, b_ref, o_ref):
    c = c_ref[...]
    cond = c / (1.0 + jnp.exp(-c))
    o_ref[...] = _dot(cond.astype(BF16), w_ref[...].astype(BF16)) + b_ref[...]


def _ada(c, w_ada, b_ada):
    B = c.shape[0]
    n = w_ada.shape[1]
    bn = D_MODEL
    return pl.pallas_call(
        _ada_kernel,
        out_shape=jax.ShapeDtypeStruct((B, n), F32),
        grid=(n // bn,),
        in_specs=[
            pl.BlockSpec((B, D_MODEL), lambda j: (0, 0)),
            pl.BlockSpec((D_MODEL, bn), lambda j: (0, j)),
            pl.BlockSpec((1, bn), lambda j: (0, j)),
        ],
        out_specs=pl.BlockSpec((B, bn), lambda j: (0, j)),
        name="ada",
    )(c, w_ada, b_ada.reshape(1, n))


def _inproj_kernel(x_ref, mod_ref, g1_ref, w_ref, wglr_ref, wgate_ref, bgate_ref, ones_ref,
                   qg_ref, kg_ref,
                   gq_ref, gk_ref, la_ref, gv_ref, gr_ref, aq_ref, ak_ref, av_ref):
    subs = [slice(i * INPROJ_SUB, (i + 1) * INPROJ_SUB) for i in range(INPROJ_TM // INPROJ_SUB)]

    def normed(rs):
        x = x_ref[rs, :]
        ms = jnp.mean(x * x, axis=-1, keepdims=True)
        y = x * lax.rsqrt(ms + EPS) * g1_ref[...]
        return (y * (1.0 + mod_ref[1:2, :]) + mod_ref[0:1, :]).astype(BF16)

    def head_norm(p, gain):
        gw = ones_ref.shape[0]
        hi, lo = _split_bf16(p * p)
        ssum = jnp.concatenate(
            [_dot(hi[:, c:c + gw], ones_ref[...]) + _dot(lo[:, c:c + gw], ones_ref[...])
             for c in range(0, ATTN_DIM, gw)], axis=-1)
        return p * lax.rsqrt(ssum * (1.0 / ATTN_HEAD_DIM) + EPS) * gain

    a0 = 2 * GLA_QK + 2 * GLA_V
    hbs = [normed(rs) for rs in subs]
    proj = lambda hb, c0, n: _dot(hb, w_ref[:, c0:c0 + n])
    early = [(_dot(hb, wglr_ref[...]), proj(hb, a0, ATTN_DIM), proj(hb, a0 + ATTN_DIM, ATTN_DIM))
             for hb in hbs]
    for rs, hb in zip(subs, hbs):
        gq_ref[rs, :] = (proj(hb, 0, GLA_QK) * (GLA_DK ** -0.5)).astype(gq_ref.dtype)
        gk_ref[rs, :] = proj(hb, GLA_QK, GLA_QK).astype(gk_ref.dtype)
        gv_ref[rs, :] = proj(hb, 2 * GLA_QK, GLA_V).astype(BF16)
        gr_ref[rs, :] = proj(hb, 2 * GLA_QK + GLA_V, GLA_V).astype(gr_ref.dtype)
        av_ref[rs, :] = proj(hb, a0 + 2 * ATTN_DIM, ATTN_DIM).astype(BF16)
    for rs, (glr, aq_raw, ak_raw) in zip(subs, early):
        z = _dot(glr.astype(BF16), wgate_ref[...]) + bgate_ref[...]
        log_sig = jnp.minimum(z, 0.0) - jnp.log1p(jnp.exp(-jnp.abs(z)))
        la_ref[rs, :] = log_sig * (1.0 / GLA_GATE_TAU)
        aq_ref[rs, :] = (head_norm(aq_raw, qg_ref[...])
                         * (ATTN_HEAD_DIM ** -0.5 * LOG2E)).astype(BF16)
        ak_ref[rs, :] = head_norm(ak_raw, kg_ref[...]).astype(BF16)


def _inproj(x, mod3, g1, w_main, w_glr, w_gate, b_gate, ones_blk, qg, kg):
    B, S, _ = x.shape
    tm = INPROJ_TM
    tok = lambda n: pl.BlockSpec((None, tm, n), lambda b, i: (b, i, 0))
    out_dims = [(GLA_QK, BF16), (GLA_QK, BF16), (GLA_QK, F32), (GLA_V, BF16), (GLA_V, BF16),
                (ATTN_DIM, BF16), (ATTN_DIM, BF16), (ATTN_DIM, BF16)]
    return pl.pallas_call(
        _inproj_kernel,
        out_shape=[jax.ShapeDtypeStruct((B, S, n), dt) for n, dt in out_dims],
        grid=(B, S // tm),
        in_specs=[
            tok(D_MODEL),
            pl.BlockSpec((None, N_MOD, D_MODEL), lambda b, i: (b, 0, 0)),
            _const_spec(g1.shape), _const_spec(w_main.shape), _const_spec(w_glr.shape),
            _const_spec(w_gate.shape), _const_spec(b_gate.shape), _const_spec(ones_blk.shape),
            _const_spec(qg.shape), _const_spec(kg.shape),
        ],
        out_specs=[tok(n) for n, _ in out_dims],
        compiler_params=pltpu.CompilerParams(
            dimension_semantics=("parallel", "parallel"), vmem_limit_bytes=VMEM_LIMIT),
        name="inproj",
    )(x, mod3, g1, w_main, w_glr, w_gate, b_gate, ones_blk, qg, kg)


def _pair_blockdiag(a, lane_lo):
    zero = jnp.zeros_like(a)
    return jnp.concatenate([jnp.where(lane_lo, a, zero), jnp.where(lane_lo, zero, a)],
                           axis=0).astype(BF16)


def _gla_kernel(q_ref, k_ref, la_ref, v_ref, r_ref, g_ref, tri_ref, o_ref,
                st_ref, st0_ref, raw_ref):
    C = GLA_CHUNK
    chunks = range(GLA_T // C)
    pairs = range(GLA_HEADS // 2)
    lanes = lambda p: slice(p * LANES, (p + 1) * LANES)
    rows = lambda c: slice(c * C, (c + 1) * C)
    vals = lambda h: slice(h * GLA_DV, (h + 1) * GLA_DV)

    @pl.when(pl.program_id(1) == 0)
    def _():
        st_ref[...] = jnp.zeros_like(st_ref)

    st0_ref[...] = st_ref[...]

    def finish(o, r, rsel, vs):
        ms = jnp.mean(o * o, axis=-1, keepdims=True)
        o = o * lax.rsqrt(ms + EPS) * g_ref[...]
        r = r.astype(F32)
        o_ref[rsel, vs] = (o * (r / (1.0 + jnp.exp(-r)))).astype(o_ref.dtype)

    row2 = lax.broadcasted_iota(jnp.int32, (2 * C, C), 0)
    col2 = lax.broadcasted_iota(jnp.int32, (2 * C, C), 1)
    causal = col2 <= jnp.where(row2 >= C, row2 - C, row2)
    lane_lo = lax.broadcasted_iota(jnp.int32, (C, LANES), 1) < GLA_DK
    lane_lo_v = lax.broadcasted_iota(jnp.int32, (GLA_DV, LANES), 1) < GLA_DK

    hi, lo = _split_bf16(la_ref[...])
    cum = [_dot(tri_ref[...], hi[rows(c), :]) + _dot(tri_ref[...], lo[rows(c), :])
           for c in chunks]

    q_state, q_in, k_in, k_out, decay = [], [], [], [], []
    for c in chunks:
        b = cum[c]
        b_last = b[C - 1:C, :]
        b_mid = b[C // 2:C // 2 + 1, :]
        q = q_ref[rows(c), :].astype(F32)
        k = k_ref[rows(c), :].astype(F32)
        qs = q * jnp.exp(b)
        qi = q * jnp.exp(b - b_mid)
        q_state.append([_pair_blockdiag(qs[:, lanes(p)], lane_lo) for p in pairs])
        q_in.append([_pair_blockdiag(qi[:, lanes(p)], lane_lo) for p in pairs])
        k_in.append((k * jnp.exp(b_mid - b)).astype(BF16))
        k_out.append((k * jnp.exp(b_last - b)).astype(BF16))
        decay.append(jnp.exp(b_last))

    scores, update = [], []
    for c in chunks:
        scores.append([_dot_nt(q_in[c][p], k_in[c][:, lanes(p)]) for p in pairs])
        update.append([[_dot_tn(v_ref[rows(c), vals(2 * p + hh)], k_out[c][:, lanes(p)])
                        for hh in range(2)] for p in pairs])
    scores = [[jnp.where(causal, s, 0.0).astype(BF16) for s in sc] for sc in scores]

    state = [st_ref[p] for p in pairs]
    seen = []
    for c in chunks:
        seen.append(state)
        state = [state[p] * decay[c][:, lanes(p)]
                 + jnp.where(lane_lo_v, update[c][p][0], update[c][p][1]) for p in pairs]
    for p in pairs:
        st_ref[p] = state[p]

    for c in chunks:
        for p in pairs:
            o_inter = _dot_nt(q_state[c][p], seen[c][p].astype(BF16))
            for hh in range(2):
                vs = vals(2 * p + hh)
                half = slice(hh * C, (hh + 1) * C)
                o = o_inter[half, :] + _dot(scores[c][p][half, :], v_ref[rows(c), vs])
                finish(o, r_ref[rows(c), vs], rows(c), vs)

    total = functools.reduce(jnp.minimum, [cum[c][C - 1:C, :] for c in chunks])
    risky = jnp.logical_not(jnp.min(total) >= -GLA_SAFE_RANGE)

    @pl.when(risky)
    def _():
        G = 16
        sub = lax.broadcasted_iota(jnp.int32, (G, LANES), 0)
        lane_lo_t = lax.broadcasted_iota(jnp.int32, (G, LANES), 1) < GLA_DK
        raw_ref[...] = jnp.zeros_like(raw_ref)

        def token(t, st):
            grp = pl.ds(pl.multiple_of(lax.shift_left(lax.shift_right_logical(t, 4), 4), G), G)
            here = sub == jnp.bitwise_and(t, G - 1)
            pick = lambda x: jnp.where(here, x, jnp.zeros_like(x))
            new = []
            for p in pairs:
                a = jnp.exp(jnp.sum(pick(la_ref[grp, lanes(p)]), axis=0, keepdims=True))
                k16 = pick(k_ref[grp, lanes(p)]).astype(BF16)
                upd = [_dot_tn(pick(v_ref[grp, vals(2 * p + hh)]), k16)
                       for hh in range(2)]
                sp = st[p] * a + jnp.where(lane_lo_v, upd[0], upd[1])
                o = _dot_nt(_pair_blockdiag(pick(q_ref[grp, lanes(p)]), lane_lo_t),
                            sp.astype(BF16))
                raw_ref[grp, vals(2 * p)] += o[:G, :]
                raw_ref[grp, vals(2 * p + 1)] += o[G:, :]
                new.append(sp)
            return tuple(new)

        st = lax.fori_loop(0, GLA_T, token, tuple(st0_ref[p] for p in pairs))
        for p in pairs:
            st_ref[p] = st[p]
        for c in chunks:
            for h in range(GLA_HEADS):
                finish(raw_ref[rows(c), vals(h)], r_ref[rows(c), vals(h)], rows(c), vals(h))


def _chunk_prefix_matrix():
    i = np.arange(GLA_CHUNK)
    return jnp.asarray(i[None, :] <= i[:, None], dtype=BF16)


def _gla(gq, gk, la, gv, gr, gain):
    B, S, _ = gq.shape
    T = GLA_T
    tri = _chunk_prefix_matrix()
    tok = lambda n: pl.BlockSpec((None, T, n), lambda b, j: (b, j, 0))
    return pl.pallas_call(
        _gla_kernel,
        out_shape=jax.ShapeDtypeStruct((B, S, GLA_V), BF16),
        grid=(B, S // T),
        in_specs=[tok(GLA_QK), tok(GLA_QK), tok(GLA_QK), tok(GLA_V), tok(GLA_V),
                  _const_spec(gain.shape), _const_spec(tri.shape)],
        out_specs=tok(GLA_V),
        scratch_shapes=[pltpu.VMEM((GLA_HEADS // 2, GLA_DV, LANES), F32),
                        pltpu.VMEM((GLA_HEADS // 2, GLA_DV, LANES), F32),
                        pltpu.VMEM((T, GLA_V), F32)],
        compiler_params=pltpu.CompilerParams(
            dimension_semantics=("parallel", "arbitrary"), vmem_limit_bytes=VMEM_LIMIT),
        name="gla",
    )(gq, gk, la, gv, gr, gain, tri)


def _attn_kernel(q_ref, k_ref, kp_ref, v_ref, vp_ref, bias_ref, y_ref, qf, kf, vf, og, mg, dg):
    SP, BL = ATTN_SPAN, ATTN_BLOCK
    first_span = pl.program_id(2) == 0
    qf[...] = q_ref[...].astype(F32)
    kf[0:SP, :] = kp_ref[...].astype(F32)
    kf[SP:, :] = k_ref[...].astype(F32)
    vf[0:SP, :] = vp_ref[...].astype(F32)
    vf[SP:, :] = v_ref[...].astype(F32)
    lane_lo = lax.broadcasted_iota(jnp.int32, (BL, LANES), 1) < ATTN_HEAD_DIM
    ones = jnp.ones((2 * BL, LANES), BF16)

    per_dilation = []
    for g, (_, d) in enumerate(DILATED_PAIRS):
        blocks = []
        for r in range(d):
            for n in range(SP // (BL * d)):
                qs = r + n * (d * BL)
                ks = qs + (SP - d * BL)
                blocks.append((g, d, qs, ks, n == 0))
        per_dilation.append(blocks)
    tiles = [t for same_index in zip(*per_dilation) for t in same_index]
    groups = [tiles[i:i + ATTN_UNROLL] for i in range(0, len(tiles), ATTN_UNROLL)]

    def score_stage(group):
        return [_dot_nt(_pair_blockdiag(qf[pl.ds(qs, BL, stride=d), :], lane_lo),
                        kf[pl.ds(ks, 2 * BL, stride=d), :].astype(BF16))
                for _, d, qs, ks, _ in group]

    def softmax_stage(group, scores):
        out = []
        for (g, _, _, _, seq_start), s in zip(group, scores):
            sel = jnp.where(first_span, 0, 1) if seq_start else 1
            s = s + bias_ref[g, sel]
            m = jnp.max(s, axis=-1, keepdims=True)
            out.append((m, jnp.exp2(s - m).astype(BF16)))
        return out

    def value_stage(group, stats):
        accs = [_dot(e, jnp.concatenate(
                    [vf[pl.ds(ks, 2 * BL, stride=d), :].astype(BF16), ones], axis=1))
                for (_, d, _, ks, _), (_, e) in zip(group, stats)]
        for (g, d, qs, _, _), (m, _), acc in zip(group, stats, accs):
            rows = pl.ds(qs, BL, stride=d)
            og[g, rows, :] = jnp.where(lane_lo, acc[:BL, :LANES], acc[BL:, :LANES])
            mg[g, rows, :] = jnp.where(lane_lo, m[:BL], m[BL:])
            dg[g, rows, :] = jnp.where(lane_lo, acc[:BL, LANES:], acc[BL:, LANES:])

    scores = score_stage(groups[0])
    for i, group in enumerate(groups):
        ahead = score_stage(groups[i + 1]) if i + 1 < len(groups) else None
        value_stage(group, softmax_stage(group, scores))
        scores = ahead

    def merge(c, carry):
        rows = pl.ds(pl.multiple_of(c * ATTN_MERGE_ROWS, ATTN_MERGE_ROWS), ATTN_MERGE_ROWS)
        ms = [mg[g, rows, :] for g in range(len(DILATED_PAIRS))]
        top = functools.reduce(jnp.maximum, ms)
        ws = [jnp.exp2(mi - top) for mi in ms]
        num = sum(w * og[g, rows, :] for g, w in enumerate(ws))
        den = sum(w * dg[g, rows, :] for g, w in enumerate(ws))
        y_ref[rows, :] = (num / den).astype(y_ref.dtype)
        return carry

    lax.fori_loop(0, SP // ATTN_MERGE_ROWS, merge, 0, unroll=4)


def _attn_bias():
    span = DILATED_PAIRS[0][0] // DILATED_PAIRS[0][1]
    slopes = np.asarray([2.0 ** (-8.0 * (h + 1) / ATTN_HEADS) for h in range(ATTN_HEADS)],
                        dtype=np.float32)
    iq = np.arange(ATTN_BLOCK)[:, None]
    ik = np.arange(2 * ATTN_BLOCK)[None, :]
    rel = iq + ATTN_BLOCK - ik
    valid = (rel >= 0) & (rel <= span)
    out = np.empty((len(DILATED_PAIRS), 2, ATTN_HEADS, ATTN_BLOCK, 2 * ATTN_BLOCK), np.float32)
    for g, (_, d) in enumerate(DILATED_PAIRS):
        for first in (0, 1):
            ok = valid & ((ik >= ATTN_BLOCK) if first == 0 else True)
            for h in range(ATTN_HEADS):
                alibi = -slopes[h] * (d * rel).astype(np.float32) * np.float32(LOG2E)
                out[g, first, h] = np.where(ok, alibi, np.float32(NEG_BIG))
    return out.reshape(len(DILATED_PAIRS), 2, ATTN_HEADS // 2, 2 * ATTN_BLOCK, 2 * ATTN_BLOCK)


def _attn(aq, ak, av):
    B, S, _ = aq.shape
    SP = ATTN_SPAN
    G = len(DILATED_PAIRS)
    bias = jnp.asarray(_attn_bias())
    cur = pl.BlockSpec((None, SP, LANES), lambda p, b, j: (b, j, p))
    prev = pl.BlockSpec((None, SP, LANES), lambda p, b, j: (b, jnp.maximum(j - 1, 0), p))
    return pl.pallas_call(
        _attn_kernel,
        out_shape=jax.ShapeDtypeStruct((B, S, ATTN_DIM), BF16),
        grid=(ATTN_HEADS // 2, B, S // SP),
        in_specs=[cur, cur, prev, cur, prev,
                  pl.BlockSpec((G, 2, None, 2 * ATTN_BLOCK, 2 * ATTN_BLOCK),
                               lambda p, b, j: (0, 0, p, 0, 0))],
        out_specs=cur,
        scratch_shapes=[pltpu.VMEM((SP, LANES), F32), pltpu.VMEM((2 * SP, LANES), F32),
                        pltpu.VMEM((2 * SP, LANES), F32), pltpu.VMEM((G, SP, LANES), F32),
                        pltpu.VMEM((G, SP, LANES), F32), pltpu.VMEM((G, SP, LANES), F32)],
        compiler_params=pltpu.CompilerParams(
            dimension_semantics=("parallel", "parallel", "parallel"),
            vmem_limit_bytes=VMEM_LIMIT),
        name="attn",
    )(aq, ak, ak, av, av, bias)


def _ffn_kernel(x_ref, yg_ref, ya_ref, mod_ref, wout_ref, g2_ref, wup_ref, cw_ref, cb_ref,
                wdn_ref, out_ref, carry_ref, act_ref):
    ts = FFN_SUB
    subs = [slice(i * ts, (i + 1) * ts) for i in range(FFN_TM // ts)]

    @pl.when(pl.program_id(1) == 0)
    def _():
        carry_ref[...] = jnp.zeros_like(carry_ref)

    rid = lax.broadcasted_iota(jnp.int32, (8, FFN_CH), 0)

    def conv(u, c0):
        cols = slice(c0, c0 + FFN_CH)
        c6 = carry_ref[6:7, cols]
        c7 = carry_ref[7:8, cols]
        r1 = pltpu.roll(u, 1, 0)
        r2 = pltpu.roll(u, 2, 0)
        u1 = jnp.concatenate([jnp.where(rid == 0, c7, r1[:8])] + [r1[8:]], axis=0)
        u2 = jnp.concatenate(
            [jnp.where(rid == 0, c6, jnp.where(rid == 1, c7, r2[:8]))] + [r2[8:]], axis=0)
        carry_ref[:, cols] = u[ts - 8:, :]
        return (cb_ref[:, cols] + cw_ref[0:1, cols] * u2 + cw_ref[1:2, cols] * u1
                + cw_ref[2:3, cols] * u)

    attn_out = [_dot(jnp.concatenate([yg_ref[rs, :], ya_ref[rs, :]], axis=-1), wout_ref[...])
                for rs in subs]
    hb = []
    for rs, a in zip(subs, attn_out):
        x1 = x_ref[rs, :] + mod_ref[2:3, :] * a
        out_ref[rs, :] = x1
        ms = jnp.mean(x1 * x1, axis=-1, keepdims=True)
        h = x1 * lax.rsqrt(ms + EPS) * g2_ref[...]
        hb.append((h * (1.0 + mod_ref[4:5, :]) + mod_ref[3:4, :]).astype(BF16))
    for rs, hs in zip(subs, hb):
        for c in range(D_FF // FFN_CH):
            g0 = c * FFN_CH
            v0 = D_FF + c * FFN_CH
            ug = conv(_dot(hs, wup_ref[:, g0:g0 + FFN_CH]), g0)
            uv = conv(_dot(hs, wup_ref[:, v0:v0 + FFN_CH]), v0)
            act_ref[rs, g0:g0 + FFN_CH] = ((ug / (1.0 + jnp.exp(-ug))) * uv).astype(BF16)
    for rs in subs:
        out_ref[rs, :] = out_ref[rs, :] + mod_ref[5:6, :] * _dot(act_ref[rs, :], wdn_ref[...])


def _ffn(x, yg, ya, mod3, w_out, g2, w_up, conv_w, conv_b, w_dn):
    B, S, _ = x.shape
    tm = FFN_TM
    tok = lambda n: pl.BlockSpec((None, tm, n), lambda b, i: (b, i, 0))
    return pl.pallas_call(
        _ffn_kernel,
        out_shape=jax.ShapeDtypeStruct((B, S, D_MODEL), F32),
        grid=(B, S // tm),
        in_specs=[tok(D_MODEL), tok(GLA_V), tok(ATTN_DIM),
                  pl.BlockSpec((None, N_MOD, D_MODEL), lambda b, i: (b, 0, 0)),
                  _const_spec(w_out.shape), _const_spec(g2.shape), _const_spec(w_up.shape),
                  _const_spec(conv_w.shape), _const_spec(conv_b.shape), _const_spec(w_dn.shape)],
        out_specs=tok(D_MODEL),
        scratch_shapes=[pltpu.VMEM((8, 2 * D_FF), F32), pltpu.VMEM((tm, D_FF), BF16)],
        compiler_params=pltpu.CompilerParams(
            dimension_semantics=("parallel", "arbitrary"), vmem_limit_bytes=VMEM_LIMIT),
        name="ffn",
    )(x, yg, ya, mod3, w_out, g2, w_up, conv_w, conv_b, w_dn)


def _head_ones():
    heads = MXU_DEPTH // ATTN_HEAD_DIM
    blk = np.kron(np.eye(heads, dtype=np.float32),
                  np.ones((ATTN_HEAD_DIM, ATTN_HEAD_DIM), np.float32))
    return jnp.asarray(blk, dtype=BF16)


def kernel(x, c, w_ada, b_ada, norm1_g, w_in, gla_w_gate, gla_b_gate, gla_norm_g, q_norm_g,
           k_norm_g, w_out, norm2_g, w_up, conv_w, conv_b, w_down):
    B = x.shape[0]
    depth = w_ada.shape[0]
    ones_blk = _head_ones()
    glr0 = 2 * GLA_QK + 2 * GLA_V
    for l in range(depth):
        mod3 = _ada(c, w_ada[l], b_ada[l]).reshape(B, N_MOD, D_MODEL)
        w = w_in[l]
        w_main = jnp.concatenate([w[:, :glr0], w[:, glr0 + GLA_GATE_RANK:]], axis=1).astype(BF16)
        w_glr = jnp.pad(w[:, glr0:glr0 + GLA_GATE_RANK],
                        ((0, 0), (0, LANES - GLA_GATE_RANK))).astype(BF16)
        w_gate = jnp.pad(gla_w_gate[l], ((0, LANES - GLA_GATE_RANK), (0, 0))).astype(BF16)
        gq, gk, la, gv, gr, aq, ak, av = _inproj(
            x, mod3, norm1_g[l].reshape(1, -1), w_main, w_glr, w_gate,
            gla_b_gate[l].reshape(1, -1), ones_blk,
            jnp.tile(q_norm_g[l], ATTN_HEADS).reshape(1, -1),
            jnp.tile(k_norm_g[l], ATTN_HEADS).reshape(1, -1))
        yg = _gla(gq, gk, la, gv, gr, gla_norm_g[l].reshape(1, -1))
        ya = _attn(aq, ak, av)
        x = _ffn(x, yg, ya, mod3, w_out[l].astype(BF16), norm2_g[l].reshape(1, -1),
                 w_up[l].astype(BF16), conv_w[l], conv_b[l].reshape(1, -1),
                 w_down[l].astype(BF16))
    return x
```

```python
import functools

import jax
import jax.numpy as jnp
import numpy as np
from jax import lax
from jax.experimental import pallas as pl
from jax.experimental.pallas import tpu as pltpu

F32 = jnp.float32
BF16 = jnp.bfloat16

D_MODEL = 1024
GLA_HEADS = 4
GLA_DK = 64
GLA_DV = 128
GLA_GATE_RANK = 16
GLA_GATE_TAU = 16.0
GLA_CHUNK = 64
GLA_QK = GLA_HEADS * GLA_DK
GLA_V = GLA_HEADS * GLA_DV
ATTN_HEADS = 8
ATTN_HEAD_DIM = 64
ATTN_DIM = ATTN_HEADS * ATTN_HEAD_DIM
DILATED_PAIRS = ((128, 1), (512, 4), (2048, 16))
ATTN_BLOCK = 128
D_FF = 2816
CONV_WIDTH = 3
N_MOD = 6
EPS = 1e-6

LANES = 128
MXU_DEPTH = 256
NEG_BIG = -1e30
LOG2E = 1.4426950408889634

INPROJ_TM = 1024
INPROJ_SUB = 512
GLA_T = 512
GLA_SAFE_RANGE = 60.0
ATTN_SPAN = 2048
ATTN_FOLD = 4
ATTN_UNROLL = 4
ATTN_MERGE_ROWS = 64
FFN_TM = 1024
FFN_SUB = 512
FFN_CH = 256
VMEM_LIMIT = 56 * 1024 * 1024


def _const_spec(shape):
    nd = len(shape)
    return pl.BlockSpec(shape, lambda *_: (0,) * nd, pipeline_mode=pl.Buffered(1))


def _dot(a, b):
    return jnp.dot(a, b, preferred_element_type=F32)


def _dot_nt(a, b):
    return lax.dot_general(a, b, (((1,), (1,)), ((), ())), preferred_element_type=F32)


def _dot_tn(a, b):
    return lax.dot_general(a, b, (((0,), (0,)), ((), ())), preferred_element_type=F32)


def _split_bf16(x):
    hi = x.astype(BF16)
    lo = (x - hi.astype(F32)).astype(BF16)
    return hi, lo


def _ada_kernel(c_ref, w_ref, b_ref, o_ref):
    c = c_ref[...]
    cond = c / (1.0 + jnp.exp(-c))
    o_ref[...] = _dot(cond.astype(BF16), w_ref[...].astype(BF16)) + b_ref[...]


def _ada(c, w_ada, b_ada):
    B = c.shape[0]
    n = w_ada.shape[1]
    bn = D_MODEL
    return pl.pallas_call(
        _ada_kernel,
        out_shape=jax.ShapeDtypeStruct((B, n), F32),
        grid=(n // bn,),
        in_specs=[
            pl.BlockSpec((B, D_MODEL), lambda j: (0, 0)),
            pl.BlockSpec((D_MODEL, bn), lambda j: (0, j)),
            pl.BlockSpec((1, bn), lambda j: (0, j)),
        ],
        out_specs=pl.BlockSpec((B, bn), lambda j: (0, j)),
        name="ada",
    )(c, w_ada, b_ada.reshape(1, n))


def _inproj_kernel(x_ref, mod_ref, g1_ref, w_ref, wglr_ref, wgate_ref, bgate_ref, ones_ref,
                   qg_ref, kg_ref,
                   gq_ref, gk_ref, la_ref, gv_ref, gr_ref, aq_ref, ak_ref, av_ref):
    subs = [slice(i * INPROJ_SUB, (i + 1) * INPROJ_SUB) for i in range(INPROJ_TM // INPROJ_SUB)]

    def normed(rs):
        x = x_ref[rs, :]
        ms = jnp.mean(x * x, axis=-1, keepdims=True)
        y = x * lax.rsqrt(ms + EPS) * g1_ref[...]
        return (y * (1.0 + mod_ref[1:2, :]) + mod_ref[0:1, :]).astype(BF16)

    def head_norm(p, gain):
        gw = ones_ref.shape[0]
        hi, lo = _split_bf16(p * p)
        ssum = jnp.concatenate(
            [_dot(hi[:, c:c + gw], ones_ref[...]) + _dot(lo[:, c:c + gw], ones_ref[...])
             for c in range(0, ATTN_DIM, gw)], axis=-1)
        return p * lax.rsqrt(ssum * (1.0 / ATTN_HEAD_DIM) + EPS) * gain

    a0 = 2 * GLA_QK + 2 * GLA_V
    hbs = [normed(rs) for rs in subs]
    proj = lambda hb, c0, n: _dot(hb, w_ref[:, c0:c0 + n])
    early = [(_dot(hb, wglr_ref[...]), proj(hb, a0, ATTN_DIM), proj(hb, a0 + ATTN_DIM, ATTN_DIM))
             for hb in hbs]
    for rs, hb in zip(subs, hbs):
        gq_ref[rs, :] = (proj(hb, 0, GLA_QK) * (GLA_DK ** -0.5)).astype(gq_ref.dtype)
        gk_ref[rs, :] = proj(hb, GLA_QK, GLA_QK).astype(gk_ref.dtype)
        gv_ref[rs, :] = proj(hb, 2 * GLA_QK, GLA_V).astype(BF16)
        gr_ref[rs, :] = proj(hb, 2 * GLA_QK + GLA_V, GLA_V).astype(gr_ref.dtype)
        av_ref[rs, :] = proj(hb, a0 + 2 * ATTN_DIM, ATTN_DIM).astype(BF16)
    for rs, (glr, aq_raw, ak_raw) in zip(subs, early):
        z = _dot(glr.astype(BF16), wgate_ref[...]) + bgate_ref[...]
        log_sig = jnp.minimum(z, 0.0) - jnp.log1p(jnp.exp(-jnp.abs(z)))
        la_ref[rs, :] = log_sig * (1.0 / GLA_GATE_TAU)
        aq_ref[rs, :] = (head_norm(aq_raw, qg_ref[...])
                         * (ATTN_HEAD_DIM ** -0.5 * LOG2E)).astype(BF16)
        ak_ref[rs, :] = head_norm(ak_raw, kg_ref[...]).astype(BF16)


def _inproj(x, mod3, g1, w_main, w_glr, w_gate, b_gate, ones_blk, qg, kg):
    B, S, _ = x.shape
    tm = INPROJ_TM
    tok = lambda n: pl.BlockSpec((None, tm, n), lambda b, i: (b, i, 0))
    out_dims = [(GLA_QK, BF16), (GLA_QK, BF16), (GLA_QK, F32), (GLA_V, BF16), (GLA_V, BF16),
                (ATTN_DIM, BF16), (ATTN_DIM, BF16), (ATTN_DIM, BF16)]
    return pl.pallas_call(
        _inproj_kernel,
        out_shape=[jax.ShapeDtypeStruct((B, S, n), dt) for n, dt in out_dims],
        grid=(B, S // tm),
        in_specs=[
            tok(D_MODEL),
            pl.BlockSpec((None, N_MOD, D_MODEL), lambda b, i: (b, 0, 0)),
            _const_spec(g1.shape), _const_spec(w_main.shape), _const_spec(w_glr.shape),
            _const_spec(w_gate.shape), _const_spec(b_gate.shape), _const_spec(ones_blk.shape),
            _const_spec(qg.shape), _const_spec(kg.shape),
        ],
        out_specs=[tok(n) for n, _ in out_dims],
        compiler_params=pltpu.CompilerParams(
            dimension_semantics=("parallel", "parallel"), vmem_limit_bytes=VMEM_LIMIT),
        name="inproj",
    )(x, mod3, g1, w_main, w_glr, w_gate, b_gate, ones_blk, qg, kg)


def _pair_blockdiag(a, lane_lo):
    zero = jnp.zeros_like(a)
    return jnp.concatenate([jnp.where(lane_lo, a, zero), jnp.where(lane_lo, zero, a)],
                           axis=0).astype(BF16)


def _gla_kernel(q_ref, k_ref, la_ref, v_ref, r_ref, g_ref, tri_ref, o_ref,
                st_ref, st0_ref, raw_ref):
    C = GLA_CHUNK
    chunks = range(GLA_T // C)
    pairs = range(GLA_HEADS // 2)
    lanes = lambda p: slice(p * LANES, (p + 1) * LANES)
    rows = lambda c: slice(c * C, (c + 1) * C)
    vals = lambda h: slice(h * GLA_DV, (h + 1) * GLA_DV)

    @pl.when(pl.program_id(1) == 0)
    def _():
        st_ref[...] = jnp.zeros_like(st_ref)

    st0_ref[...] = st_ref[...]

    def finish(o, r, rsel, vs):
        ms = jnp.mean(o * o, axis=-1, keepdims=True)
        o = o * lax.rsqrt(ms + EPS) * g_ref[...]
        r = r.astype(F32)
        o_ref[rsel, vs] = (o * (r / (1.0 + jnp.exp(-r)))).astype(o_ref.dtype)

    row2 = lax.broadcasted_iota(jnp.int32, (2 * C, C), 0)
    col2 = lax.broadcasted_iota(jnp.int32, (2 * C, C), 1)
    causal = col2 <= jnp.where(row2 >= C, row2 - C, row2)
    lane_lo = lax.broadcasted_iota(jnp.int32, (C, LANES), 1) < GLA_DK
    lane_lo_v = lax.broadcasted_iota(jnp.int32, (GLA_DV, LANES), 1) < GLA_DK

    hi, lo = _split_bf16(la_ref[...])
    cum = [_dot(tri_ref[...], hi[rows(c), :]) + _dot(tri_ref[...], lo[rows(c), :])
           for c in chunks]

    q_state, q_in, k_in, k_out, decay = [], [], [], [], []
    for c in chunks:
        b = cum[c]
        b_last = b[C - 1:C, :]
        b_mid = b[C // 2:C // 2 + 1, :]
        q = q_ref[rows(c), :].astype(F32)
        k = k_ref[rows(c), :].astype(F32)
        qs = q * jnp.exp(b)
        qi = q * jnp.exp(b - b_mid)
        q_state.append([_pair_blockdiag(qs[:, lanes(p)], lane_lo) for p in pairs])
        q_in.append([_pair_blockdiag(qi[:, lanes(p)], lane_lo) for p in pairs])
        k_in.append((k * jnp.exp(b_mid - b)).astype(BF16))
        k_out.append((k * jnp.exp(b_last - b)).astype(BF16))
        decay.append(jnp.exp(b_last))

    scores, update = [], []
    for c in chunks:
        scores.append([_dot_nt(q_in[c][p], k_in[c][:, lanes(p)]) for p in pairs])
        update.append([[_dot_tn(v_ref[rows(c), vals(2 * p + hh)], k_out[c][:, lanes(p)])
                        for hh in range(2)] for p in pairs])
    scores = [[jnp.where(causal, s, 0.0).astype(BF16) for s in sc] for sc in scores]

    state = [st_ref[p] for p in pairs]
    seen = []
    for c in chunks:
        seen.append(state)
        state = [state[p] * decay[c][:, lanes(p)]
                 + jnp.where(lane_lo_v, update[c][p][0], update[c][p][1]) for p in pairs]
    for p in pairs:
        st_ref[p] = state[p]

    for c in chunks:
        for p in pairs:
            o_inter = _dot_nt(q_state[c][p], seen[c][p].astype(BF16))
            for hh in range(2):
                vs = vals(2 * p + hh)
                half = slice(hh * C, (hh + 1) * C)
                o = o_inter[half, :] + _dot(scores[c][p][half, :], v_ref[rows(c), vs])
                finish(o, r_ref[rows(c), vs], rows(c), vs)

    total = functools.reduce(jnp.minimum, [cum[c][C - 1:C, :] for c in chunks])
    risky = jnp.logical_not(jnp.min(total) >= -GLA_SAFE_RANGE)

    @pl.when(risky)
    def _():
        G = 16
        sub = lax.broadcasted_iota(jnp.int32, (G, LANES), 0)
        lane_lo_t = lax.broadcasted_iota(jnp.int32, (G, LANES), 1) < GLA_DK
        raw_ref[...] = jnp.zeros_like(raw_ref)

        def token(t, st):
            grp = pl.ds(pl.multiple_of(lax.shift_left(lax.shift_right_logical(t, 4), 4), G), G)
            here = sub == jnp.bitwise_and(t, G - 1)
            pick = lambda x: jnp.where(here, x, jnp.zeros_like(x))
            new = []
            for p in pairs:
                a = jnp.exp(jnp.sum(pick(la_ref[grp, lanes(p)]), axis=0, keepdims=True))
                k16 = pick(k_ref[grp, lanes(p)]).astype(BF16)
                upd = [_dot_tn(pick(v_ref[grp, vals(2 * p + hh)]), k16)
                       for hh in range(2)]
                sp = st[p] * a + jnp.where(lane_lo_v, upd[0], upd[1])
                o = _dot_nt(_pair_blockdiag(pick(q_ref[grp, lanes(p)]), lane_lo_t),
                            sp.astype(BF16))
                raw_ref[grp, vals(2 * p)] += o[:G, :]
                raw_ref[grp, vals(2 * p + 1)] += o[G:, :]
                new.append(sp)
            return tuple(new)

        st = lax.fori_loop(0, GLA_T, token, tuple(st0_ref[p] for p in pairs))
        for p in pairs:
            st_ref[p] = st[p]
        for c in chunks:
            for h in range(GLA_HEADS):
                finish(raw_ref[rows(c), vals(h)], r_ref[rows(c), vals(h)], rows(c), vals(h))


def _chunk_prefix_matrix():
    i = np.arange(GLA_CHUNK)
    return jnp.asarray(i[None, :] <= i[:, None], dtype=BF16)


def _gla(gq, gk, la, gv, gr, gain):
    B, S, _ = gq.shape
    T = GLA_T
    tri = _chunk_prefix_matrix()
    tok = lambda n: pl.BlockSpec((None, T, n), lambda b, j: (b, j, 0))
    return pl.pallas_call(
        _gla_kernel,
        out_shape=jax.ShapeDtypeStruct((B, S, GLA_V), BF16),
        grid=(B, S // T),
        in_specs=[tok(GLA_QK), tok(GLA_QK), tok(GLA_QK), tok(GLA_V), tok(GLA_V),
                  _const_spec(gain.shape), _const_spec(tri.shape)],
        out_specs=tok(GLA_V),
        scratch_shapes=[pltpu.VMEM((GLA_HEADS // 2, GLA_DV, LANES), F32),
                        pltpu.VMEM((GLA_HEADS // 2, GLA_DV, LANES), F32),
                        pltpu.VMEM((T, GLA_V), F32)],
        compiler_params=pltpu.CompilerParams(
            dimension_semantics=("parallel", "arbitrary"), vmem_limit_bytes=VMEM_LIMIT),
        name="gla",
    )(gq, gk, la, gv, gr, gain, tri)


def _attn_kernel(q_ref, k_ref, kp_ref, v_ref, vp_ref, bias_ref, y_ref,
                 qf, kf, vf, q4, k4, v4, og, mg, dg, ys):
    SP, BL, F = ATTN_SPAN, ATTN_BLOCK, ATTN_FOLD
    QR, KR = SP // F, 2 * SP // F
    first_span = pl.program_id(2) == 0
    qf[...] = q_ref[...].astype(F32)
    kf[0:SP, :] = kp_ref[...].astype(F32)
    kf[SP:, :] = k_ref[...].astype(F32)
    vf[0:SP, :] = vp_ref[...].astype(F32)
    vf[SP:, :] = v_ref[...].astype(F32)
    for r in range(F):
        q4[r * QR:(r + 1) * QR, :] = qf[pl.ds(r, QR, stride=F), :]
        k4[r * KR:(r + 1) * KR, :] = kf[pl.ds(r, KR, stride=F), :]
        v4[r * KR:(r + 1) * KR, :] = vf[pl.ds(r, KR, stride=F), :]
    lane_lo = lax.broadcasted_iota(jnp.int32, (BL, LANES), 1) < ATTN_HEAD_DIM
    ones = jnp.ones((2 * BL, LANES), BF16)

    (_, d0), (_, d1), (_, d2) = DILATED_PAIRS
    assert d0 == 1 and d1 == F and d2 % F == 0 and SP == BL * d2
    s2 = d2 // F
    load = lambda buf, rows: (lambda: buf[rows, :].astype(BF16))
    across = lambda prev, cur: (lambda: jnp.concatenate([prev[SP - BL:SP, :], cur[0:BL, :]], axis=0))
    per_dilation = [[], [], []]
    for n in range(SP // BL):
        keys = pl.ds((n - 1) * BL, 2 * BL)
        per_dilation[0].append((0, load(q_ref, pl.ds(n * BL, BL)),
                                load(k_ref, keys) if n else across(kp_ref, k_ref),
                                load(v_ref, keys) if n else across(vp_ref, v_ref),
                                pl.ds(n * BL, BL), n == 0))
    for r in range(F):
        for n in range(QR // BL):
            keys = pl.ds(r * KR + QR + (n - 1) * BL, 2 * BL)
            per_dilation[1].append((1, load(q4, pl.ds(r * QR + n * BL, BL)), load(k4, keys),
                                    load(v4, keys), pl.ds(r * QR + n * BL, BL), n == 0))
    for r in range(F):
        for a in range(s2):
            keys = pl.ds(r * KR + a, 2 * BL, stride=s2)
            rows = pl.ds(r * QR + a, BL, stride=s2)
            per_dilation[2].append((2, load(q4, rows), load(k4, keys), load(v4, keys), rows, True))
    tiles = [t for same_index in zip(*per_dilation) for t in same_index]
    groups = [tiles[i:i + ATTN_UNROLL] for i in range(0, len(tiles), ATTN_UNROLL)]

    def score_stage(group):
        return [_dot_nt(_pair_blockdiag(q(), lane_lo), k()) for _, q, k, _, _, _ in group]

    def softmax_stage(group, scores):
        out = []
        for (g, *_, seq_start), s in zip(group, scores):
            sel = jnp.where(first_span, 0, 1) if seq_start else 1
            s = s + bias_ref[g, sel]
            m = jnp.max(s, axis=-1, keepdims=True)
            out.append((m, jnp.exp2(s - m).astype(BF16)))
        return out

    def value_stage(group, stats):
        accs = [_dot(e, jnp.concatenate([v(), ones], axis=1))
                for (_, _, _, v, _, _), (_, e) in zip(group, stats)]
        for (g, _, _, _, rows, _), (m, _), acc in zip(group, stats, accs):
            og[g, rows, :] = jnp.where(lane_lo, acc[:BL, :LANES], acc[BL:, :LANES])
            mg[g, rows, :] = jnp.where(lane_lo, m[:BL], m[BL:])
            dg[g, rows, :] = jnp.where(lane_lo, acc[:BL, LANES:], acc[BL:, LANES:])

    scores = score_stage(groups[0])
    for i, group in enumerate(groups):
        ahead = score_stage(groups[i + 1]) if i + 1 < len(groups) else None
        value_stage(group, softmax_stage(group, scores))
        scores = ahead

    R = ATTN_MERGE_ROWS

    def merge(c, carry):
        res = pl.ds(pl.multiple_of(c * R, R), R)
        r = c // (QR // R)
        j0 = (c - r * (QR // R)) * R
        tok = pl.ds(r + F * j0, R, stride=F)
        rows = (tok, res, res)
        ms = [mg[g, rows[g], :] for g in range(3)]
        top = functools.reduce(jnp.maximum, ms)
        ws = [jnp.exp2(mi - top) for mi in ms]
        num = sum(w * og[g, rows[g], :] for g, w in enumerate(ws))
        den = sum(w * dg[g, rows[g], :] for g, w in enumerate(ws))
        ys[tok, :] = num / den
        return carry

    lax.fori_loop(0, SP // R, merge, 0, unroll=4)
    y_ref[...] = ys[...].astype(y_ref.dtype)


def _attn_bias():
    span = DILATED_PAIRS[0][0] // DILATED_PAIRS[0][1]
    slopes = np.asarray([2.0 ** (-8.0 * (h + 1) / ATTN_HEADS) for h in range(ATTN_HEADS)],
                        dtype=np.float32)
    iq = np.arange(ATTN_BLOCK)[:, None]
    ik = np.arange(2 * ATTN_BLOCK)[None, :]
    rel = iq + ATTN_BLOCK - ik
    valid = (rel >= 0) & (rel <= span)
    out = np.empty((len(DILATED_PAIRS), 2, ATTN_HEADS, ATTN_BLOCK, 2 * ATTN_BLOCK), np.float32)
    for g, (_, d) in enumerate(DILATED_PAIRS):
        for first in (0, 1):
            ok = valid & ((ik >= ATTN_BLOCK) if first == 0 else True)
            for h in range(ATTN_HEADS):
                alibi = -slopes[h] * (d * rel).astype(np.float32) * np.float32(LOG2E)
                out[g, first, h] = np.where(ok, alibi, np.float32(NEG_BIG))
    return out.reshape(len(DILATED_PAIRS), 2, ATTN_HEADS // 2, 2 * ATTN_BLOCK, 2 * ATTN_BLOCK)


def _attn(aq, ak, av):
    B, S, _ = aq.shape
    SP = ATTN_SPAN
    G = len(DILATED_PAIRS)
    bias = jnp.asarray(_attn_bias())
    cur = pl.BlockSpec((None, SP, LANES), lambda p, b, j: (b, j, p))
    prev = pl.BlockSpec((None, SP, LANES), lambda p, b, j: (b, jnp.maximum(j - 1, 0), p))
    return pl.pallas_call(
        _attn_kernel,
        out_shape=jax.ShapeDtypeStruct((B, S, ATTN_DIM), BF16),
        grid=(ATTN_HEADS // 2, B, S // SP),
        in_specs=[cur, cur, prev, cur, prev,
                  pl.BlockSpec((G, 2, None, 2 * ATTN_BLOCK, 2 * ATTN_BLOCK),
                               lambda p, b, j: (0, 0, p, 0, 0))],
        out_specs=cur,
        scratch_shapes=[pltpu.VMEM((SP, LANES), F32), pltpu.VMEM((2 * SP, LANES), F32),
                        pltpu.VMEM((2 * SP, LANES), F32), pltpu.VMEM((SP, LANES), F32),
                        pltpu.VMEM((2 * SP, LANES), F32), pltpu.VMEM((2 * SP, LANES), F32),
                        pltpu.VMEM((G, SP, LANES), F32), pltpu.VMEM((G, SP, LANES), F32),
                        pltpu.VMEM((G, SP, LANES), F32), pltpu.VMEM((SP, LANES), F32)],
        compiler_params=pltpu.CompilerParams(
            dimension_semantics=("parallel", "parallel", "parallel"),
            vmem_limit_bytes=VMEM_LIMIT),
        name="attn",
    )(aq, ak, ak, av, av, bias)


def _ffn_kernel(x_ref, yg_ref, ya_ref, mod_ref, wout_ref, g2_ref, wup_ref, cw_ref, cb_ref,
                wdn_ref, out_ref, carry_ref, act_ref):
    ts = FFN_SUB
    subs = [slice(i * ts, (i + 1) * ts) for i in range(FFN_TM // ts)]

    @pl.when(pl.program_id(1) == 0)
    def _():
        carry_ref[...] = jnp.zeros_like(carry_ref)

    rid = lax.broadcasted_iota(jnp.int32, (8, FFN_CH), 0)

    def conv(u, c0):
        cols = slice(c0, c0 + FFN_CH)
        c6 = carry_ref[6:7, cols]
        c7 = carry_ref[7:8, cols]
        r1 = pltpu.roll(u, 1, 0)
        r2 = pltpu.roll(u, 2, 0)
        u1 = jnp.concatenate([jnp.where(rid == 0, c7, r1[:8])] + [r1[8:]], axis=0)
        u2 = jnp.concatenate(
            [jnp.where(rid == 0, c6, jnp.where(rid == 1, c7, r2[:8]))] + [r2[8:]], axis=0)
        carry_ref[:, cols] = u[ts - 8:, :]
        return (cb_ref[:, cols] + cw_ref[0:1, cols] * u2 + cw_ref[1:2, cols] * u1
                + cw_ref[2:3, cols] * u)

    attn_out = [_dot(jnp.concatenate([yg_ref[rs, :], ya_ref[rs, :]], axis=-1), wout_ref[...])
                for rs in subs]
    hb = []
    for rs, a in zip(subs, attn_out):
        x1 = x_ref[rs, :] + mod_ref[2:3, :] * a
        out_ref[rs, :] = x1
        ms = jnp.mean(x1 * x1, axis=-1, keepdims=True)
        h = x1 * lax.rsqrt(ms + EPS) * g2_ref[...]
        hb.append((h * (1.0 + mod_ref[4:5, :]) + mod_ref[3:4, :]).astype(BF16))
    for rs, hs in zip(subs, hb):
        for c in range(D_FF // FFN_CH):
            g0 = c * FFN_CH
            v0 = D_FF + c * FFN_CH
            ug = conv(_dot(hs, wup_ref[:, g0:g0 + FFN_CH]), g0)
            uv = conv(_dot(hs, wup_ref[:, v0:v0 + FFN_CH]), v0)
            act_ref[rs, g0:g0 + FFN_CH] = ((ug / (1.0 + jnp.exp(-ug))) * uv).astype(BF16)
    for rs in subs:
        out_ref[rs, :] = out_ref[rs, :] + mod_ref[5:6, :] * _dot(act_ref[rs, :], wdn_ref[...])


def _ffn(x, yg, ya, mod3, w_out, g2, w_up, conv_w, conv_b, w_dn):
    B, S, _ = x.shape
    tm = FFN_TM
    tok = lambda n: pl.BlockSpec((None, tm, n), lambda b, i: (b, i, 0))
    return pl.pallas_call(
        _ffn_kernel,
        out_shape=jax.ShapeDtypeStruct((B, S, D_MODEL), F32),
        grid=(B, S // tm),
        in_specs=[tok(D_MODEL), tok(GLA_V), tok(ATTN_DIM),
                  pl.BlockSpec((None, N_MOD, D_MODEL), lambda b, i: (b, 0, 0)),
                  _const_spec(w_out.shape), _const_spec(g2.shape), _const_spec(w_up.shape),
                  _const_spec(conv_w.shape), _const_spec(conv_b.shape), _const_spec(w_dn.shape)],
        out_specs=tok(D_MODEL),
        scratch_shapes=[pltpu.VMEM((8, 2 * D_FF), F32), pltpu.VMEM((tm, D_FF), BF16)],
        compiler_params=pltpu.CompilerParams(
            dimension_semantics=("parallel", "arbitrary"), vmem_limit_bytes=VMEM_LIMIT),
        name="ffn",
    )(x, yg, ya, mod3, w_out, g2, w_up, conv_w, conv_b, w_dn)


def _head_ones():
    heads = MXU_DEPTH // ATTN_HEAD_DIM
    blk = np.kron(np.eye(heads, dtype=np.float32),
                  np.ones((ATTN_HEAD_DIM, ATTN_HEAD_DIM), np.float32))
    return jnp.asarray(blk, dtype=BF16)


def kernel(x, c, w_ada, b_ada, norm1_g, w_in, gla_w_gate, gla_b_gate, gla_norm_g, q_norm_g,
           k_norm_g, w_out, norm2_g, w_up, conv_w, conv_b, w_down):
    B = x.shape[0]
    depth = w_ada.shape[0]
    ones_blk = _head_ones()
    glr0 = 2 * GLA_QK + 2 * GLA_V
    for l in range(depth):
        mod3 = _ada(c, w_ada[l], b_ada[l]).reshape(B, N_MOD, D_MODEL)
        w = w_in[l]
        w_main = jnp.concatenate([w[:, :glr0], w[:, glr0 + GLA_GATE_RANK:]], axis=1).astype(BF16)
        w_glr = jnp.pad(w[:, glr0:glr0 + GLA_GATE_RANK],
                        ((0, 0), (0, LANES - GLA_GATE_RANK))).astype(BF16)
        w_gate = jnp.pad(gla_w_gate[l], ((0, LANES - GLA_GATE_RANK), (0, 0))).astype(BF16)
        gq, gk, la, gv, gr, aq, ak, av = _inproj(
            x, mod3, norm1_g[l].reshape(1, -1), w_main, w_glr, w_gate,
            gla_b_gate[l].reshape(1, -1), ones_blk,
            jnp.tile(q_norm_g[l], ATTN_HEADS).reshape(1, -1),
            jnp.tile(k_norm_g[l], ATTN_HEADS).reshape(1, -1))
        yg = _gla(gq, gk, la, gv, gr, gla_norm_g[l].reshape(1, -1))
        ya = _attn(aq, ak, av)
        x = _ffn(x, yg, ya, mod3, w_out[l].astype(BF16), norm2_g[l].reshape(1, -1),
                 w_up[l].astype(BF16), conv_w[l], conv_b[l].reshape(1, -1),
                 w_down[l].astype(BF16))
    return x
```

```python
import functools

import jax
import jax.numpy as jnp
import numpy as np
from jax import lax
from jax.experimental import pallas as pl
from jax.experimental.pallas import tpu as pltpu

F32 = jnp.float32
BF16 = jnp.bfloat16

D_MODEL = 1024
GLA_HEADS = 4
GLA_DK = 64
GLA_DV = 128
GLA_GATE_RANK = 16
GLA_GATE_TAU = 16.0
GLA_CHUNK = 64
GLA_QK = GLA_HEADS * GLA_DK
GLA_V = GLA_HEADS * GLA_DV
ATTN_HEADS = 8
ATTN_HEAD_DIM = 64
ATTN_DIM = ATTN_HEADS * ATTN_HEAD_DIM
DILATED_PAIRS = ((128, 1), (512, 4), (2048, 16))
ATTN_BLOCK = 128
D_FF = 2816
CONV_WIDTH = 3
N_MOD = 6
EPS = 1e-6

LANES = 128
MXU_DEPTH = 256
NEG_BIG = -1e30
LOG2E = 1.4426950408889634

INPROJ_TM = 1024
INPROJ_SUB = 512
GLA_T = 1024
GLA_SAFE_RANGE = 60.0
ATTN_SPAN = 2048
ATTN_FOLD = 4
ATTN_UNROLL = 4
ATTN_MERGE_ROWS = 64
FFN_TM = 1024
FFN_SUB = 512
FFN_CH = 256
VMEM_LIMIT = 56 * 1024 * 1024


def _const_spec(shape):
    nd = len(shape)
    return pl.BlockSpec(shape, lambda *_: (0,) * nd, pipeline_mode=pl.Buffered(1))


def _dot(a, b):
    return jnp.dot(a, b, preferred_element_type=F32)


def _dot_nt(a, b):
    return lax.dot_general(a, b, (((1,), (1,)), ((), ())), preferred_element_type=F32)


def _dot_tn(a, b):
    return lax.dot_general(a, b, (((0,), (0,)), ((), ())), preferred_element_type=F32)


def _split_bf16(x):
    hi = x.astype(BF16)
    lo = (x - hi.astype(F32)).astype(BF16)
    return hi, lo


def _ada_kernel(c_ref, w_ref, b_ref, o_ref):
    c = c_ref[...]
    cond = c / (1.0 + jnp.exp(-c))
    o_ref[...] = _dot(cond.astype(BF16), w_ref[...].astype(BF16)) + b_ref[...]


def _ada(c, w_ada, b_ada):
    B = c.shape[0]
    n = w_ada.shape[1]
    bn = D_MODEL
    return pl.pallas_call(
        _ada_kernel,
        out_shape=jax.ShapeDtypeStruct((B, n), F32),
        grid=(n // bn,),
        in_specs=[
            pl.BlockSpec((B, D_MODEL), lambda j: (0, 0)),
            pl.BlockSpec((D_MODEL, bn), lambda j: (0, j)),
            pl.BlockSpec((1, bn), lambda j: (0, j)),
        ],
        out_specs=pl.BlockSpec((B, bn), lambda j: (0, j)),
        name="ada",
    )(c, w_ada, b_ada.reshape(1, n))


def _inproj_kernel(x_ref, mod_ref, g1_ref, w_ref, wglr_ref, wgate_ref, bgate_ref, ones_ref,
                   qg_ref, kg_ref,
                   gq_ref, gk_ref, la_ref, gv_ref, gr_ref, aq_ref, ak_ref, av_ref):
    subs = [slice(i * INPROJ_SUB, (i + 1) * INPROJ_SUB) for i in range(INPROJ_TM // INPROJ_SUB)]

    def normed(rs):
        x = x_ref[rs, :]
        ms = jnp.mean(x * x, axis=-1, keepdims=True)
        y = x * lax.rsqrt(ms + EPS) * g1_ref[...]
        return (y * (1.0 + mod_ref[1:2, :]) + mod_ref[0:1, :]).astype(BF16)

    def head_norm(p, gain):
        gw = ones_ref.shape[0]
        hi, lo = _split_bf16(p * p)
        ssum = jnp.concatenate(
            [_dot(hi[:, c:c + gw], ones_ref[...]) + _dot(lo[:, c:c + gw], ones_ref[...])
             for c in range(0, ATTN_DIM, gw)], axis=-1)
        return p * lax.rsqrt(ssum * (1.0 / ATTN_HEAD_DIM) + EPS) * gain

    a0 = 2 * GLA_QK + 2 * GLA_V
    hbs = [normed(rs) for rs in subs]
    proj = lambda hb, c0, n: _dot(hb, w_ref[:, c0:c0 + n])
    early = [(_dot(hb, wglr_ref[...]), proj(hb, a0, ATTN_DIM), proj(hb, a0 + ATTN_DIM, ATTN_DIM))
             for hb in hbs]
    for rs, hb in zip(subs, hbs):
        gq_ref[rs, :] = (proj(hb, 0, GLA_QK) * (GLA_DK ** -0.5)).astype(gq_ref.dtype)
        gk_ref[rs, :] = proj(hb, GLA_QK, GLA_QK).astype(gk_ref.dtype)
        gv_ref[rs, :] = proj(hb, 2 * GLA_QK, GLA_V).astype(BF16)
        gr_ref[rs, :] = proj(hb, 2 * GLA_QK + GLA_V, GLA_V).astype(gr_ref.dtype)
        av_ref[rs, :] = proj(hb, a0 + 2 * ATTN_DIM, ATTN_DIM).astype(BF16)
    for rs, (glr, aq_raw, ak_raw) in zip(subs, early):
        z = _dot(glr.astype(BF16), wgate_ref[...]) + bgate_ref[...]
        log_sig = jnp.minimum(z, 0.0) - jnp.log1p(jnp.exp(-jnp.abs(z)))
        la_ref[rs, :] = log_sig * (1.0 / GLA_GATE_TAU)
        aq_ref[rs, :] = (head_norm(aq_raw, qg_ref[...])
                         * (ATTN_HEAD_DIM ** -0.5 * LOG2E)).astype(BF16)
        ak_ref[rs, :] = head_norm(ak_raw, kg_ref[...]).astype(BF16)


def _inproj(x, mod3, g1, w_main, w_glr, w_gate, b_gate, ones_blk, qg, kg):
    B, S, _ = x.shape
    tm = INPROJ_TM
    tok = lambda n: pl.BlockSpec((None, tm, n), lambda b, i: (b, i, 0))
    out_dims = [(GLA_QK, BF16), (GLA_QK, BF16), (GLA_QK, F32), (GLA_V, BF16), (GLA_V, BF16),
                (ATTN_DIM, BF16), (ATTN_DIM, BF16), (ATTN_DIM, BF16)]
    return pl.pallas_call(
        _inproj_kernel,
        out_shape=[jax.ShapeDtypeStruct((B, S, n), dt) for n, dt in out_dims],
        grid=(B, S // tm),
        in_specs=[
            tok(D_MODEL),
            pl.BlockSpec((None, N_MOD, D_MODEL), lambda b, i: (b, 0, 0)),
            _const_spec(g1.shape), _const_spec(w_main.shape), _const_spec(w_glr.shape),
            _const_spec(w_gate.shape), _const_spec(b_gate.shape), _const_spec(ones_blk.shape),
            _const_spec(qg.shape), _const_spec(kg.shape),
        ],
        out_specs=[tok(n) for n, _ in out_dims],
        compiler_params=pltpu.CompilerParams(
            dimension_semantics=("parallel", "parallel"), vmem_limit_bytes=VMEM_LIMIT),
        name="inproj",
    )(x, mod3, g1, w_main, w_glr, w_gate, b_gate, ones_blk, qg, kg)


def _pair_blockdiag(a, lane_lo):
    zero = jnp.zeros_like(a)
    return jnp.concatenate([jnp.where(lane_lo, a, zero), jnp.where(lane_lo, zero, a)],
                           axis=0).astype(BF16)


def _gla_kernel(q_ref, k_ref, la_ref, v_ref, r_ref, g_ref, tri_ref, o_ref,
                st_ref, st0_ref, raw_ref):
    C = GLA_CHUNK
    chunks = range(GLA_T // C)
    pairs = range(GLA_HEADS // 2)
    lanes = lambda p: slice(p * LANES, (p + 1) * LANES)
    rows = lambda c: slice(c * C, (c + 1) * C)
    vals = lambda h: slice(h * GLA_DV, (h + 1) * GLA_DV)

    @pl.when(pl.program_id(1) == 0)
    def _():
        st_ref[...] = jnp.zeros_like(st_ref)

    st0_ref[...] = st_ref[...]

    def finish(o, r, rsel, vs):
        ms = jnp.mean(o * o, axis=-1, keepdims=True)
        o = o * lax.rsqrt(ms + EPS) * g_ref[...]
        r = r.astype(F32)
        o_ref[rsel, vs] = (o * (r / (1.0 + jnp.exp(-r)))).astype(o_ref.dtype)

    row2 = lax.broadcasted_iota(jnp.int32, (2 * C, C), 0)
    col2 = lax.broadcasted_iota(jnp.int32, (2 * C, C), 1)
    causal = col2 <= jnp.where(row2 >= C, row2 - C, row2)
    lane_lo = lax.broadcasted_iota(jnp.int32, (C, LANES), 1) < GLA_DK
    lane_lo_v = lax.broadcasted_iota(jnp.int32, (GLA_DV, LANES), 1) < GLA_DK

    hi, lo = _split_bf16(la_ref[...])
    cum = [_dot(tri_ref[...], hi[rows(c), :]) + _dot(tri_ref[...], lo[rows(c), :])
           for c in chunks]

    q_state, q_in, k_in, k_out, decay = [], [], [], [], []
    for c in chunks:
        b = cum[c]
        b_last = b[C - 1:C, :]
        b_mid = b[C // 2:C // 2 + 1, :]
        q = q_ref[rows(c), :].astype(F32)
        k = k_ref[rows(c), :].astype(F32)
        qs = q * jnp.exp(b)
        qi = q * jnp.exp(b - b_mid)
        q_state.append([_pair_blockdiag(qs[:, lanes(p)], lane_lo) for p in pairs])
        q_in.append([_pair_blockdiag(qi[:, lanes(p)], lane_lo) for p in pairs])
        k_in.append((k * jnp.exp(b_mid - b)).astype(BF16))
        k_out.append((k * jnp.exp(b_last - b)).astype(BF16))
        decay.append(jnp.exp(b_last))

    scores, update = [], []
    for c in chunks:
        scores.append([_dot_nt(q_in[c][p], k_in[c][:, lanes(p)]) for p in pairs])
        update.append([[_dot_tn(v_ref[rows(c), vals(2 * p + hh)], k_out[c][:, lanes(p)])
                        for hh in range(2)] for p in pairs])
    scores = [[jnp.where(causal, s, 0.0).astype(BF16) for s in sc] for sc in scores]

    state = [st_ref[p] for p in pairs]
    seen = []
    for c in chunks:
        seen.append(state)
        state = [state[p] * decay[c][:, lanes(p)]
                 + jnp.where(lane_lo_v, update[c][p][0], update[c][p][1]) for p in pairs]
    for p in pairs:
        st_ref[p] = state[p]

    for c in chunks:
        for p in pairs:
            o_inter = _dot_nt(q_state[c][p], seen[c][p].astype(BF16))
            for hh in range(2):
                vs = vals(2 * p + hh)
                half = slice(hh * C, (hh + 1) * C)
                o = o_inter[half, :] + _dot(scores[c][p][half, :], v_ref[rows(c), vs])
                finish(o, r_ref[rows(c), vs], rows(c), vs)

    total = functools.reduce(jnp.minimum, [cum[c][C - 1:C, :] for c in chunks])
    risky = jnp.logical_not(jnp.min(total) >= -GLA_SAFE_RANGE)

    @pl.when(risky)
    def _():
        G = 16
        sub = lax.broadcasted_iota(jnp.int32, (G, LANES), 0)
        lane_lo_t = lax.broadcasted_iota(jnp.int32, (G, LANES), 1) < GLA_DK
        raw_ref[...] = jnp.zeros_like(raw_ref)

        def token(t, st):
            grp = pl.ds(pl.multiple_of(lax.shift_left(lax.shift_right_logical(t, 4), 4), G), G)
            here = sub == jnp.bitwise_and(t, G - 1)
            pick = lambda x: jnp.where(here, x, jnp.zeros_like(x))
            new = []
            for p in pairs:
                a = jnp.exp(jnp.sum(pick(la_ref[grp, lanes(p)]), axis=0, keepdims=True))
                k16 = pick(k_ref[grp, lanes(p)]).astype(BF16)
                upd = [_dot_tn(pick(v_ref[grp, vals(2 * p + hh)]), k16)
                       for hh in range(2)]
                sp = st[p] * a + jnp.where(lane_lo_v, upd[0], upd[1])
                o = _dot_nt(_pair_blockdiag(pick(q_ref[grp, lanes(p)]), lane_lo_t),
                            sp.astype(BF16))
                raw_ref[grp, vals(2 * p)] += o[:G, :]
                raw_ref[grp, vals(2 * p + 1)] += o[G:, :]
                new.append(sp)
            return tuple(new)

        st = lax.fori_loop(0, GLA_T, token, tuple(st0_ref[p] for p in pairs))
        for p in pairs:
            st_ref[p] = st[p]
        for c in chunks:
            for h in range(GLA_HEADS):
                finish(raw_ref[rows(c), vals(h)], r_ref[rows(c), vals(h)], rows(c), vals(h))


def _chunk_prefix_matrix():
    i = np.arange(GLA_CHUNK)
    return jnp.asarray(i[None, :] <= i[:, None], dtype=BF16)


def _gla(gq, gk, la, gv, gr, gain):
    B, S, _ = gq.shape
    T = GLA_T
    tri = _chunk_prefix_matrix()
    tok = lambda n: pl.BlockSpec((None, T, n), lambda b, j: (b, j, 0))
    return pl.pallas_call(
        _gla_kernel,
        out_shape=jax.ShapeDtypeStruct((B, S, GLA_V), BF16),
        grid=(B, S // T),
        in_specs=[tok(GLA_QK), tok(GLA_QK), tok(GLA_QK), tok(GLA_V), tok(GLA_V),
                  _const_spec(gain.shape), _const_spec(tri.shape)],
        out_specs=tok(GLA_V),
        scratch_shapes=[pltpu.VMEM((GLA_HEADS // 2, GLA_DV, LANES), F32),
                        pltpu.VMEM((GLA_HEADS // 2, GLA_DV, LANES), F32),
                        pltpu.VMEM((T, GLA_V), F32)],
        compiler_params=pltpu.CompilerParams(
            dimension_semantics=("parallel", "arbitrary"), vmem_limit_bytes=VMEM_LIMIT),
        name="gla",
    )(gq, gk, la, gv, gr, gain, tri)


def _attn_kernel(q_ref, k_ref, kp_ref, v_ref, vp_ref, bias_ref, y_ref,
                 qf, kf, vf, q4, k4, v4, og, mg, dg, ys):
    SP, BL, F = ATTN_SPAN, ATTN_BLOCK, ATTN_FOLD
    QR, KR = SP // F, 2 * SP // F
    first_span = pl.program_id(2) == 0
    qf[...] = q_ref[...].astype(F32)
    kf[0:SP, :] = kp_ref[...].astype(F32)
    kf[SP:, :] = k_ref[...].astype(F32)
    vf[0:SP, :] = vp_ref[...].astype(F32)
    vf[SP:, :] = v_ref[...].astype(F32)
    for r in range(F):
        q4[r * QR:(r + 1) * QR, :] = qf[pl.ds(r, QR, stride=F), :]
        k4[r * KR:(r + 1) * KR, :] = kf[pl.ds(r, KR, stride=F), :]
        v4[r * KR:(r + 1) * KR, :] = vf[pl.ds(r, KR, stride=F), :]
    lane_lo = lax.broadcasted_iota(jnp.int32, (BL, LANES), 1) < ATTN_HEAD_DIM
    ones = jnp.ones((2 * BL, LANES), BF16)

    (_, d0), (_, d1), (_, d2) = DILATED_PAIRS
    assert d0 == 1 and d1 == F and d2 % F == 0 and SP == BL * d2
    s2 = d2 // F
    load = lambda buf, rows: (lambda: buf[rows, :].astype(BF16))
    across = lambda prev, cur: (lambda: jnp.concatenate([prev[SP - BL:SP, :], cur[0:BL, :]], axis=0))
    per_dilation = [[], [], []]
    for n in range(SP // BL):
        keys = pl.ds((n - 1) * BL, 2 * BL)
        per_dilation[0].append((0, load(q_ref, pl.ds(n * BL, BL)),
                                load(k_ref, keys) if n else across(kp_ref, k_ref),
                                load(v_ref, keys) if n else across(vp_ref, v_ref),
                                pl.ds(n * BL, BL), n == 0))
    for r in range(F):
        for n in range(QR // BL):
            keys = pl.ds(r * KR + QR + (n - 1) * BL, 2 * BL)
            per_dilation[1].append((1, load(q4, pl.ds(r * QR + n * BL, BL)), load(k4, keys),
                                    load(v4, keys), pl.ds(r * QR + n * BL, BL), n == 0))
    for r in range(F):
        for a in range(s2):
            keys = pl.ds(r * KR + a, 2 * BL, stride=s2)
            rows = pl.ds(r * QR + a, BL, stride=s2)
            per_dilation[2].append((2, load(q4, rows), load(k4, keys), load(v4, keys), rows, True))
    tiles = [t for same_index in zip(*per_dilation) for t in same_index]
    groups = [tiles[i:i + ATTN_UNROLL] for i in range(0, len(tiles), ATTN_UNROLL)]

    def score_stage(group):
        return [_dot_nt(_pair_blockdiag(q(), lane_lo), k()) for _, q, k, _, _, _ in group]

    def softmax_stage(group, scores):
        out = []
        for (g, *_, seq_start), s in zip(group, scores):
            sel = jnp.where(first_span, 0, 1) if seq_start else 1
            s = s + bias_ref[g, sel]
            m = jnp.max(s, axis=-1, keepdims=True)
            out.append((m, jnp.exp2(s - m).astype(BF16)))
        return out

    def value_stage(group, stats):
        accs = [_dot(e, jnp.concatenate([v(), ones], axis=1))
                for (_, _, _, v, _, _), (_, e) in zip(group, stats)]
        for (g, _, _, _, rows, _), (m, _), acc in zip(group, stats, accs):
            og[g, rows, :] = jnp.where(lane_lo, acc[:BL, :LANES], acc[BL:, :LANES])
            mg[g, rows, :] = jnp.where(lane_lo, m[:BL], m[BL:])
            dg[g, rows, :] = jnp.where(lane_lo, acc[:BL, LANES:], acc[BL:, LANES:])

    scores = score_stage(groups[0])
    for i, group in enumerate(groups):
        ahead = score_stage(groups[i + 1]) if i + 1 < len(groups) else None
        value_stage(group, softmax_stage(group, scores))
        scores = ahead

    R = ATTN_MERGE_ROWS

    def merge(c, carry):
        res = pl.ds(pl.multiple_of(c * R, R), R)
        r = c // (QR // R)
        j0 = (c - r * (QR // R)) * R
        tok = pl.ds(r + F * j0, R, stride=F)
        rows = (tok, res, res)
        ms = [mg[g, rows[g], :] for g in range(3)]
        top = functools.reduce(jnp.maximum, ms)
        ws = [jnp.exp2(mi - top) for mi in ms]
        num = sum(w * og[g, rows[g], :] for g, w in enumerate(ws))
        den = sum(w * dg[g, rows[g], :] for g, w in enumerate(ws))
        ys[tok, :] = num / den
        return carry

    lax.fori_loop(0, SP // R, merge, 0, unroll=4)
    y_ref[...] = ys[...].astype(y_ref.dtype)


def _attn_bias():
    span = DILATED_PAIRS[0][0] // DILATED_PAIRS[0][1]
    slopes = np.asarray([2.0 ** (-8.0 * (h + 1) / ATTN_HEADS) for h in range(ATTN_HEADS)],
                        dtype=np.float32)
    iq = np.arange(ATTN_BLOCK)[:, None]
    ik = np.arange(2 * ATTN_BLOCK)[None, :]
    rel = iq + ATTN_BLOCK - ik
    valid = (rel >= 0) & (rel <= span)
    out = np.empty((len(DILATED_PAIRS), 2, ATTN_HEADS, ATTN_BLOCK, 2 * ATTN_BLOCK), np.float32)
    for g, (_, d) in enumerate(DILATED_PAIRS):
        for first in (0, 1):
            ok = valid & ((ik >= ATTN_BLOCK) if first == 0 else True)
            for h in range(ATTN_HEADS):
                alibi = -slopes[h] * (d * rel).astype(np.float32) * np.float32(LOG2E)
                out[g, first, h] = np.where(ok, alibi, np.float32(NEG_BIG))
    return out.reshape(len(DILATED_PAIRS), 2, ATTN_HEADS // 2, 2 * ATTN_BLOCK, 2 * ATTN_BLOCK)


def _attn(aq, ak, av):
    B, S, _ = aq.shape
    SP = ATTN_SPAN
    G = len(DILATED_PAIRS)
    bias = jnp.asarray(_attn_bias())
    cur = pl.BlockSpec((None, SP, LANES), lambda p, b, j: (b, j, p))
    prev = pl.BlockSpec((None, SP, LANES), lambda p, b, j: (b, jnp.maximum(j - 1, 0), p))
    return pl.pallas_call(
        _attn_kernel,
        out_shape=jax.ShapeDtypeStruct((B, S, ATTN_DIM), BF16),
        grid=(ATTN_HEADS // 2, B, S // SP),
        in_specs=[cur, cur, prev, cur, prev,
                  pl.BlockSpec((G, 2, None, 2 * ATTN_BLOCK, 2 * ATTN_BLOCK),
                               lambda p, b, j: (0, 0, p, 0, 0))],
        out_specs=cur,
        scratch_shapes=[pltpu.VMEM((SP, LANES), F32), pltpu.VMEM((2 * SP, LANES), F32),
                        pltpu.VMEM((2 * SP, LANES), F32), pltpu.VMEM((SP, LANES), F32),
                        pltpu.VMEM((2 * SP, LANES), F32), pltpu.VMEM((2 * SP, LANES), F32),
                        pltpu.VMEM((G, SP, LANES), F32), pltpu.VMEM((G, SP, LANES), F32),
                        pltpu.VMEM((G, SP, LANES), F32), pltpu.VMEM((SP, LANES), F32)],
        compiler_params=pltpu.CompilerParams(
            dimension_semantics=("parallel", "parallel", "parallel"),
            vmem_limit_bytes=VMEM_LIMIT),
        name="attn",
    )(aq, ak, ak, av, av, bias)


def _ffn_kernel(x_ref, yg_ref, ya_ref, mod_ref, wout_ref, g2_ref, wup_ref, cw_ref, cb_ref,
                wdn_ref, out_ref, carry_ref, act_ref):
    ts = FFN_SUB
    subs = [slice(i * ts, (i + 1) * ts) for i in range(FFN_TM // ts)]

    @pl.when(pl.program_id(1) == 0)
    def _():
        carry_ref[...] = jnp.zeros_like(carry_ref)

    rid = lax.broadcasted_iota(jnp.int32, (8, FFN_CH), 0)

    def conv(u, c0):
        cols = slice(c0, c0 + FFN_CH)
        c6 = carry_ref[6:7, cols]
        c7 = carry_ref[7:8, cols]
        r1 = pltpu.roll(u, 1, 0)
        r2 = pltpu.roll(u, 2, 0)
        u1 = jnp.concatenate([jnp.where(rid == 0, c7, r1[:8])] + [r1[8:]], axis=0)
        u2 = jnp.concatenate(
            [jnp.where(rid == 0, c6, jnp.where(rid == 1, c7, r2[:8]))] + [r2[8:]], axis=0)
        carry_ref[:, cols] = u[ts - 8:, :]
        return (cb_ref[:, cols] + cw_ref[0:1, cols] * u2 + cw_ref[1:2, cols] * u1
                + cw_ref[2:3, cols] * u)

    attn_out = [_dot(jnp.concatenate([yg_ref[rs, :], ya_ref[rs, :]], axis=-1), wout_ref[...])
                for rs in subs]
    hb = []
    for rs, a in zip(subs, attn_out):
        x1 = x_ref[rs, :] + mod_ref[2:3, :] * a
        out_ref[rs, :] = x1
        ms = jnp.mean(x1 * x1, axis=-1, keepdims=True)
        h = x1 * lax.rsqrt(ms + EPS) * g2_ref[...]
        hb.append((h * (1.0 + mod_ref[4:5, :]) + mod_ref[3:4, :]).astype(BF16))
    for rs, hs in zip(subs, hb):
        for c in range(D_FF // FFN_CH):
            g0 = c * FFN_CH
            v0 = D_FF + c * FFN_CH
            ug = conv(_dot(hs, wup_ref[:, g0:g0 + FFN_CH]), g0)
            uv = conv(_dot(hs, wup_ref[:, v0:v0 + FFN_CH]), v0)
            act_ref[rs, g0:g0 + FFN_CH] = ((ug / (1.0 + jnp.exp(-ug))) * uv).astype(BF16)
    for rs in subs:
        out_ref[rs, :] = out_ref[rs, :] + mod_ref[5:6, :] * _dot(act_ref[rs, :], wdn_ref[...])


def _ffn(x, yg, ya, mod3, w_out, g2, w_up, conv_w, conv_b, w_dn):
    B, S, _ = x.shape
    tm = FFN_TM
    tok = lambda n: pl.BlockSpec((None, tm, n), lambda b, i: (b, i, 0))
    return pl.pallas_call(
        _ffn_kernel,
        out_shape=jax.ShapeDtypeStruct((B, S, D_MODEL), F32),
        grid=(B, S // tm),
        in_specs=[tok(D_MODEL), tok(GLA_V), tok(ATTN_DIM),
                  pl.BlockSpec((None, N_MOD, D_MODEL), lambda b, i: (b, 0, 0)),
                  _const_spec(w_out.shape), _const_spec(g2.shape), _const_spec(w_up.shape),
                  _const_spec(conv_w.shape), _const_spec(conv_b.shape), _const_spec(w_dn.shape)],
        out_specs=tok(D_MODEL),
        scratch_shapes=[pltpu.VMEM((8, 2 * D_FF), F32), pltpu.VMEM((tm, D_FF), BF16)],
        compiler_params=pltpu.CompilerParams(
            dimension_semantics=("parallel", "arbitrary"), vmem_limit_bytes=VMEM_LIMIT),
        name="ffn",
    )(x, yg, ya, mod3, w_out, g2, w_up, conv_w, conv_b, w_dn)


def _head_ones():
    heads = MXU_DEPTH // ATTN_HEAD_DIM
    blk = np.kron(np.eye(heads, dtype=np.float32),
                  np.ones((ATTN_HEAD_DIM, ATTN_HEAD_DIM), np.float32))
    return jnp.asarray(blk, dtype=BF16)


def kernel(x, c, w_ada, b_ada, norm1_g, w_in, gla_w_gate, gla_b_gate, gla_norm_g, q_norm_g,
           k_norm_g, w_out, norm2_g, w_up, conv_w, conv_b, w_down):
    B = x.shape[0]
    depth = w_ada.shape[0]
    ones_blk = _head_ones()
    glr0 = 2 * GLA_QK + 2 * GLA_V
    for l in range(depth):
        mod3 = _ada(c, w_ada[l], b_ada[l]).reshape(B, N_MOD, D_MODEL)
        w = w_in[l]
        w_main = jnp.concatenate([w[:, :glr0], w[:, glr0 + GLA_GATE_RANK:]], axis=1).astype(BF16)
        w_glr = jnp.pad(w[:, glr0:glr0 + GLA_GATE_RANK],
                        ((0, 0), (0, LANES - GLA_GATE_RANK))).astype(BF16)
        w_gate = jnp.pad(gla_w_gate[l], ((0, LANES - GLA_GATE_RANK), (0, 0))).astype(BF16)
        gq, gk, la, gv, gr, aq, ak, av = _inproj(
            x, mod3, norm1_g[l].reshape(1, -1), w_main, w_glr, w_gate,
            gla_b_gate[l].reshape(1, -1), ones_blk,
            jnp.tile(q_norm_g[l], ATTN_HEADS).reshape(1, -1),
            jnp.tile(k_norm_g[l], ATTN_HEADS).reshape(1, -1))
        yg = _gla(gq, gk, la, gv, gr, gla_norm_g[l].reshape(1, -1))
        ya = _attn(aq, ak, av)
        x = _ffn(x, yg, ya, mod3, w_out[l].astype(BF16), norm2_g[l].reshape(1, -1),
                 w_up[l].astype(BF16), conv_w[l], conv_b[l].reshape(1, -1),
                 w_down[l].astype(BF16))
    return x
```

```python
import functools

import jax
import jax.numpy as jnp
import numpy as np
from jax import lax
from jax.experimental import pallas as pl
from jax.experimental.pallas import tpu as pltpu

F32 = jnp.float32
BF16 = jnp.bfloat16

D_MODEL = 1024
GLA_HEADS = 4
GLA_DK = 64
GLA_DV = 128
GLA_GATE_RANK = 16
GLA_GATE_TAU = 16.0
GLA_CHUNK = 64
GLA_QK = GLA_HEADS * GLA_DK
GLA_V = GLA_HEADS * GLA_DV
ATTN_HEADS = 8
ATTN_HEAD_DIM = 64
ATTN_DIM = ATTN_HEADS * ATTN_HEAD_DIM
DILATED_PAIRS = ((128, 1), (512, 4), (2048, 16))
ATTN_BLOCK = 128
D_FF = 2816
CONV_WIDTH = 3
N_MOD = 6
EPS = 1e-6

LANES = 128
MXU_DEPTH = 256
NEG_BIG = -1e30
LOG2E = 1.4426950408889634

INPROJ_TM = 1024
INPROJ_SUB = 512
GLA_T = 1024
GLA_SAFE_RANGE = 60.0
ATTN_SPAN = 2048
ATTN_FOLD = 4
ATTN_UNROLL = 6
ATTN_MERGE_ROWS = 64
FFN_TM = 1024
FFN_SUB = 512
FFN_CH = 256
VMEM_LIMIT = 56 * 1024 * 1024


def _const_spec(shape):
    nd = len(shape)
    return pl.BlockSpec(shape, lambda *_: (0,) * nd, pipeline_mode=pl.Buffered(1))


def _dot(a, b):
    return jnp.dot(a, b, preferred_element_type=F32)


def _dot_nt(a, b):
    return lax.dot_general(a, b, (((1,), (1,)), ((), ())), preferred_element_type=F32)


def _dot_tn(a, b):
    return lax.dot_general(a, b, (((0,), (0,)), ((), ())), preferred_element_type=F32)


def _split_bf16(x):
    hi = x.astype(BF16)
    lo = (x - hi.astype(F32)).astype(BF16)
    return hi, lo


def _ada_kernel(c_ref, w_ref, b_ref, o_ref):
    c = c_ref[...]
    cond = c / (1.0 + jnp.exp(-c))
    o_ref[...] = _dot(cond.astype(BF16), w_ref[...].astype(BF16)) + b_ref[...]


def _ada(c, w_ada, b_ada):
    B = c.shape[0]
    n = w_ada.shape[1]
    bn = D_MODEL
    return pl.pallas_call(
        _ada_kernel,
        out_shape=jax.ShapeDtypeStruct((B, n), F32),
        grid=(n // bn,),
        in_specs=[
            pl.BlockSpec((B, D_MODEL), lambda j: (0, 0)),
            pl.BlockSpec((D_MODEL, bn), lambda j: (0, j)),
            pl.BlockSpec((1, bn), lambda j: (0, j)),
        ],
        out_specs=pl.BlockSpec((B, bn), lambda j: (0, j)),
        name="ada",
    )(c, w_ada, b_ada.reshape(1, n))


def _inproj_kernel(x_ref, mod_ref, g1_ref, w_ref, wglr_ref, wgate_ref, bgate_ref, ones_ref,
                   qg_ref, kg_ref,
                   gq_ref, gk_ref, la_ref, gv_ref, gr_ref, aq_ref, ak_ref, av_ref):
    subs = [slice(i * INPROJ_SUB, (i + 1) * INPROJ_SUB) for i in range(INPROJ_TM // INPROJ_SUB)]

    def normed(rs):
        x = x_ref[rs, :]
        ms = jnp.mean(x * x, axis=-1, keepdims=True)
        y = x * lax.rsqrt(ms + EPS) * g1_ref[...]
        return (y * (1.0 + mod_ref[1:2, :]) + mod_ref[0:1, :]).astype(BF16)

    def head_norm(p, gain):
        gw = ones_ref.shape[0]
        hi, lo = _split_bf16(p * p)
        ssum = jnp.concatenate(
            [_dot(hi[:, c:c + gw], ones_ref[...]) + _dot(lo[:, c:c + gw], ones_ref[...])
             for c in range(0, ATTN_DIM, gw)], axis=-1)
        return p * lax.rsqrt(ssum * (1.0 / ATTN_HEAD_DIM) + EPS) * gain

    a0 = 2 * GLA_QK + 2 * GLA_V
    hbs = [normed(rs) for rs in subs]
    proj = lambda hb, c0, n: _dot(hb, w_ref[:, c0:c0 + n])
    early = [(_dot(hb, wglr_ref[...]), proj(hb, a0, ATTN_DIM), proj(hb, a0 + ATTN_DIM, ATTN_DIM))
             for hb in hbs]
    for rs, hb in zip(subs, hbs):
        gq_ref[rs, :] = (proj(hb, 0, GLA_QK) * (GLA_DK ** -0.5)).astype(gq_ref.dtype)
        gk_ref[rs, :] = proj(hb, GLA_QK, GLA_QK).astype(gk_ref.dtype)
        gv_ref[rs, :] = proj(hb, 2 * GLA_QK, GLA_V).astype(BF16)
        gr_ref[rs, :] = proj(hb, 2 * GLA_QK + GLA_V, GLA_V).astype(gr_ref.dtype)
        av_ref[rs, :] = proj(hb, a0 + 2 * ATTN_DIM, ATTN_DIM).astype(BF16)
    for rs, (glr, aq_raw, ak_raw) in zip(subs, early):
        z = _dot(glr.astype(BF16), wgate_ref[...]) + bgate_ref[...]
        log_sig = jnp.minimum(z, 0.0) - jnp.log1p(jnp.exp(-jnp.abs(z)))
        la_ref[rs, :] = log_sig * (1.0 / GLA_GATE_TAU)
        aq_ref[rs, :] = (head_norm(aq_raw, qg_ref[...])
                         * (ATTN_HEAD_DIM ** -0.5 * LOG2E)).astype(BF16)
        ak_ref[rs, :] = head_norm(ak_raw, kg_ref[...]).astype(BF16)


def _inproj(x, mod3, g1, w_main, w_glr, w_gate, b_gate, ones_blk, qg, kg):
    B, S, _ = x.shape
    tm = INPROJ_TM
    tok = lambda n: pl.BlockSpec((None, tm, n), lambda b, i: (b, i, 0))
    out_dims = [(GLA_QK, BF16), (GLA_QK, BF16), (GLA_QK, F32), (GLA_V, BF16), (GLA_V, BF16),
                (ATTN_DIM, BF16), (ATTN_DIM, BF16), (ATTN_DIM, BF16)]
    return pl.pallas_call(
        _inproj_kernel,
        out_shape=[jax.ShapeDtypeStruct((B, S, n), dt) for n, dt in out_dims],
        grid=(B, S // tm),
        in_specs=[
            tok(D_MODEL),
            pl.BlockSpec((None, N_MOD, D_MODEL), lambda b, i: (b, 0, 0)),
            _const_spec(g1.shape), _const_spec(w_main.shape), _const_spec(w_glr.shape),
            _const_spec(w_gate.shape), _const_spec(b_gate.shape), _const_spec(ones_blk.shape),
            _const_spec(qg.shape), _const_spec(kg.shape),
        ],
        out_specs=[tok(n) for n, _ in out_dims],
        compiler_params=pltpu.CompilerParams(
            dimension_semantics=("parallel", "parallel"), vmem_limit_bytes=VMEM_LIMIT),
        name="inproj",
    )(x, mod3, g1, w_main, w_glr, w_gate, b_gate, ones_blk, qg, kg)


def _pair_blockdiag(a, lane_lo):
    zero = jnp.zeros_like(a)
    return jnp.concatenate([jnp.where(lane_lo, a, zero), jnp.where(lane_lo, zero, a)],
                           axis=0).astype(BF16)


def _gla_kernel(q_ref, k_ref, la_ref, v_ref, r_ref, g_ref, tri_ref, o_ref,
                st_ref, st0_ref, raw_ref):
    C = GLA_CHUNK
    chunks = range(GLA_T // C)
    pairs = range(GLA_HEADS // 2)
    lanes = lambda p: slice(p * LANES, (p + 1) * LANES)
    rows = lambda c: slice(c * C, (c + 1) * C)
    vals = lambda h: slice(h * GLA_DV, (h + 1) * GLA_DV)

    @pl.when(pl.program_id(1) == 0)
    def _():
        st_ref[...] = jnp.zeros_like(st_ref)

    st0_ref[...] = st_ref[...]

    def finish(o, r, rsel, vs):
        ms = jnp.mean(o * o, axis=-1, keepdims=True)
        o = o * lax.rsqrt(ms + EPS) * g_ref[...]
        r = r.astype(F32)
        o_ref[rsel, vs] = (o * (r / (1.0 + jnp.exp(-r)))).astype(o_ref.dtype)

    row2 = lax.broadcasted_iota(jnp.int32, (2 * C, C), 0)
    col2 = lax.broadcasted_iota(jnp.int32, (2 * C, C), 1)
    causal = col2 <= jnp.where(row2 >= C, row2 - C, row2)
    lane_lo = lax.broadcasted_iota(jnp.int32, (C, LANES), 1) < GLA_DK
    lane_lo_v = lax.broadcasted_iota(jnp.int32, (GLA_DV, LANES), 1) < GLA_DK

    hi, lo = _split_bf16(la_ref[...])
    cum = [_dot(tri_ref[...], hi[rows(c), :]) + _dot(tri_ref[...], lo[rows(c), :])
           for c in chunks]

    q_state, q_in, k_in, k_out, decay = [], [], [], [], []
    for c in chunks:
        b = cum[c]
        b_last = b[C - 1:C, :]
        b_mid = b[C // 2:C // 2 + 1, :]
        q = q_ref[rows(c), :].astype(F32)
        k = k_ref[rows(c), :].astype(F32)
        qs = q * jnp.exp(b)
        qi = q * jnp.exp(b - b_mid)
        q_state.append([_pair_blockdiag(qs[:, lanes(p)], lane_lo) for p in pairs])
        q_in.append([_pair_blockdiag(qi[:, lanes(p)], lane_lo) for p in pairs])
        k_in.append((k * jnp.exp(b_mid - b)).astype(BF16))
        k_out.append((k * jnp.exp(b_last - b)).astype(BF16))
        decay.append(jnp.exp(b_last))

    scores, update = [], []
    for c in chunks:
        scores.append([_dot_nt(q_in[c][p], k_in[c][:, lanes(p)]) for p in pairs])
        update.append([[_dot_tn(v_ref[rows(c), vals(2 * p + hh)], k_out[c][:, lanes(p)])
                        for hh in range(2)] for p in pairs])
    scores = [[jnp.where(causal, s, 0.0).astype(BF16) for s in sc] for sc in scores]

    state = [st_ref[p] for p in pairs]
    seen = []
    for c in chunks:
        seen.append(state)
        state = [state[p] * decay[c][:, lanes(p)]
                 + jnp.where(lane_lo_v, update[c][p][0], update[c][p][1]) for p in pairs]
    for p in pairs:
        st_ref[p] = state[p]

    for c in chunks:
        for p in pairs:
            o_inter = _dot_nt(q_state[c][p], seen[c][p].astype(BF16))
            for hh in range(2):
                vs = vals(2 * p + hh)
                half = slice(hh * C, (hh + 1) * C)
                o = o_inter[half, :] + _dot(scores[c][p][half, :], v_ref[rows(c), vs])
                finish(o, r_ref[rows(c), vs], rows(c), vs)

    total = functools.reduce(jnp.minimum, [cum[c][C - 1:C, :] for c in chunks])
    risky = jnp.logical_not(jnp.min(total) >= -GLA_SAFE_RANGE)

    @pl.when(risky)
    def _():
        G = 16
        sub = lax.broadcasted_iota(jnp.int32, (G, LANES), 0)
        lane_lo_t = lax.broadcasted_iota(jnp.int32, (G, LANES), 1) < GLA_DK
        raw_ref[...] = jnp.zeros_like(raw_ref)

        def token(t, st):
            grp = pl.ds(pl.multiple_of(lax.shift_left(lax.shift_right_logical(t, 4), 4), G), G)
            here = sub == jnp.bitwise_and(t, G - 1)
            pick = lambda x: jnp.where(here, x, jnp.zeros_like(x))
            new = []
            for p in pairs:
                a = jnp.exp(jnp.sum(pick(la_ref[grp, lanes(p)]), axis=0, keepdims=True))
                k16 = pick(k_ref[grp, lanes(p)]).astype(BF16)
                upd = [_dot_tn(pick(v_ref[grp, vals(2 * p + hh)]), k16)
                       for hh in range(2)]
                sp = st[p] * a + jnp.where(lane_lo_v, upd[0], upd[1])
                o = _dot_nt(_pair_blockdiag(pick(q_ref[grp, lanes(p)]), lane_lo_t),
                            sp.astype(BF16))
                raw_ref[grp, vals(2 * p)] += o[:G, :]
                raw_ref[grp, vals(2 * p + 1)] += o[G:, :]
                new.append(sp)
            return tuple(new)

        st = lax.fori_loop(0, GLA_T, token, tuple(st0_ref[p] for p in pairs))
        for p in pairs:
            st_ref[p] = st[p]
        for c in chunks:
            for h in range(GLA_HEADS):
                finish(raw_ref[rows(c), vals(h)], r_ref[rows(c), vals(h)], rows(c), vals(h))


def _chunk_prefix_matrix():
    i = np.arange(GLA_CHUNK)
    return jnp.asarray(i[None, :] <= i[:, None], dtype=BF16)


def _gla(gq, gk, la, gv, gr, gain):
    B, S, _ = gq.shape
    T = GLA_T
    tri = _chunk_prefix_matrix()
    tok = lambda n: pl.BlockSpec((None, T, n), lambda b, j: (b, j, 0))
    return pl.pallas_call(
        _gla_kernel,
        out_shape=jax.ShapeDtypeStruct((B, S, GLA_V), BF16),
        grid=(B, S // T),
        in_specs=[tok(GLA_QK), tok(GLA_QK), tok(GLA_QK), tok(GLA_V), tok(GLA_V),
                  _const_spec(gain.shape), _const_spec(tri.shape)],
        out_specs=tok(GLA_V),
        scratch_shapes=[pltpu.VMEM((GLA_HEADS // 2, GLA_DV, LANES), F32),
                        pltpu.VMEM((GLA_HEADS // 2, GLA_DV, LANES), F32),
                        pltpu.VMEM((T, GLA_V), F32)],
        compiler_params=pltpu.CompilerParams(
            dimension_semantics=("parallel", "arbitrary"), vmem_limit_bytes=VMEM_LIMIT),
        name="gla",
    )(gq, gk, la, gv, gr, gain, tri)


def _attn_kernel(q_ref, k_ref, kp_ref, v_ref, vp_ref, bias_ref, y_ref,
                 qf, kf, vf, q4, k4, v4, og, mg, dg, ys):
    SP, BL, F = ATTN_SPAN, ATTN_BLOCK, ATTN_FOLD
    QR, KR = SP // F, 2 * SP // F
    first_span = pl.program_id(2) == 0
    qf[...] = q_ref[...].astype(F32)
    kf[0:SP, :] = kp_ref[...].astype(F32)
    kf[SP:, :] = k_ref[...].astype(F32)
    vf[0:SP, :] = vp_ref[...].astype(F32)
    vf[SP:, :] = v_ref[...].astype(F32)
    for r in range(F):
        q4[r * QR:(r + 1) * QR, :] = qf[pl.ds(r, QR, stride=F), :]
        k4[r * KR:(r + 1) * KR, :] = kf[pl.ds(r, KR, stride=F), :]
        v4[r * KR:(r + 1) * KR, :] = vf[pl.ds(r, KR, stride=F), :]
    lane_lo = lax.broadcasted_iota(jnp.int32, (BL, LANES), 1) < ATTN_HEAD_DIM
    ones = jnp.ones((2 * BL, LANES), BF16)

    (_, d0), (_, d1), (_, d2) = DILATED_PAIRS
    assert d0 == 1 and d1 == F and d2 % F == 0 and SP == BL * d2
    s2 = d2 // F
    load = lambda buf, rows: (lambda: buf[rows, :].astype(BF16))
    across = lambda prev, cur: (lambda: jnp.concatenate([prev[SP - BL:SP, :], cur[0:BL, :]], axis=0))
    per_dilation = [[], [], []]
    for n in range(SP // BL):
        keys = pl.ds((n - 1) * BL, 2 * BL)
        per_dilation[0].append((0, load(q_ref, pl.ds(n * BL, BL)),
                                load(k_ref, keys) if n else across(kp_ref, k_ref),
                                load(v_ref, keys) if n else across(vp_ref, v_ref),
                                pl.ds(n * BL, BL), n == 0))
    for r in range(F):
        for n in range(QR // BL):
            keys = pl.ds(r * KR + QR + (n - 1) * BL, 2 * BL)
            per_dilation[1].append((1, load(q4, pl.ds(r * QR + n * BL, BL)), load(k4, keys),
                                    load(v4, keys), pl.ds(r * QR + n * BL, BL), n == 0))
    for r in range(F):
        for a in range(s2):
            keys = pl.ds(r * KR + a, 2 * BL, stride=s2)
            rows = pl.ds(r * QR + a, BL, stride=s2)
            per_dilation[2].append((2, load(q4, rows), load(k4, keys), load(v4, keys), rows, True))
    tiles = [t for same_index in zip(*per_dilation) for t in same_index]
    groups = [tiles[i:i + ATTN_UNROLL] for i in range(0, len(tiles), ATTN_UNROLL)]

    def score_stage(group):
        return [_dot_nt(_pair_blockdiag(q(), lane_lo), k()) for _, q, k, _, _, _ in group]

    def softmax_stage(group, scores):
        out = []
        for (g, *_, seq_start), s in zip(group, scores):
            sel = jnp.where(first_span, 0, 1) if seq_start else 1
            s = s + bias_ref[g, sel]
            m = jnp.max(s, axis=-1, keepdims=True)
            out.append((m, jnp.exp2(s - m).astype(BF16)))
        return out

    def value_stage(group, stats):
        accs = [_dot(e, jnp.concatenate([v(), ones], axis=1))
                for (_, _, _, v, _, _), (_, e) in zip(group, stats)]
        for (g, _, _, _, rows, _), (m, _), acc in zip(group, stats, accs):
            og[g, rows, :] = jnp.where(lane_lo, acc[:BL, :LANES], acc[BL:, :LANES])
            mg[g, rows, :] = jnp.where(lane_lo, m[:BL], m[BL:])
            dg[g, rows, :] = jnp.where(lane_lo, acc[:BL, LANES:], acc[BL:, LANES:])

    scores = score_stage(groups[0])
    for i, group in enumerate(groups):
        ahead = score_stage(groups[i + 1]) if i + 1 < len(groups) else None
        value_stage(group, softmax_stage(group, scores))
        scores = ahead

    R = ATTN_MERGE_ROWS

    def merge(c, carry):
        res = pl.ds(pl.multiple_of(c * R, R), R)
        r = c // (QR // R)
        j0 = (c - r * (QR // R)) * R
        tok = pl.ds(r + F * j0, R, stride=F)
        rows = (tok, res, res)
        ms = [mg[g, rows[g], :] for g in range(3)]
        top = functools.reduce(jnp.maximum, ms)
        ws = [jnp.exp2(mi - top) for mi in ms]
        num = sum(w * og[g, rows[g], :] for g, w in enumerate(ws))
        den = sum(w * dg[g, rows[g], :] for g, w in enumerate(ws))
        ys[tok, :] = num / den
        return carry

    lax.fori_loop(0, SP // R, merge, 0, unroll=4)
    y_ref[...] = ys[...].astype(y_ref.dtype)


def _attn_bias():
    span = DILATED_PAIRS[0][0] // DILATED_PAIRS[0][1]
    slopes = np.asarray([2.0 ** (-8.0 * (h + 1) / ATTN_HEADS) for h in range(ATTN_HEADS)],
                        dtype=np.float32)
    iq = np.arange(ATTN_BLOCK)[:, None]
    ik = np.arange(2 * ATTN_BLOCK)[None, :]
    rel = iq + ATTN_BLOCK - ik
    valid = (rel >= 0) & (rel <= span)
    out = np.empty((len(DILATED_PAIRS), 2, ATTN_HEADS, ATTN_BLOCK, 2 * ATTN_BLOCK), np.float32)
    for g, (_, d) in enumerate(DILATED_PAIRS):
        for first in (0, 1):
            ok = valid & ((ik >= ATTN_BLOCK) if first == 0 else True)
            for h in range(ATTN_HEADS):
                alibi = -slopes[h] * (d * rel).astype(np.float32) * np.float32(LOG2E)
                out[g, first, h] = np.where(ok, alibi, np.float32(NEG_BIG))
    return out.reshape(len(DILATED_PAIRS), 2, ATTN_HEADS // 2, 2 * ATTN_BLOCK, 2 * ATTN_BLOCK)


def _attn(aq, ak, av):
    B, S, _ = aq.shape
    SP = ATTN_SPAN
    G = len(DILATED_PAIRS)
    bias = jnp.asarray(_attn_bias())
    cur = pl.BlockSpec((None, SP, LANES), lambda p, b, j: (b, j, p))
    prev = pl.BlockSpec((None, SP, LANES), lambda p, b, j: (b, jnp.maximum(j - 1, 0), p))
    return pl.pallas_call(
        _attn_kernel,
        out_shape=jax.ShapeDtypeStruct((B, S, ATTN_DIM), BF16),
        grid=(ATTN_HEADS // 2, B, S // SP),
        in_specs=[cur, cur, prev, cur, prev,
                  pl.BlockSpec((G, 2, None, 2 * ATTN_BLOCK, 2 * ATTN_BLOCK),
                               lambda p, b, j: (0, 0, p, 0, 0))],
        out_specs=cur,
        scratch_shapes=[pltpu.VMEM((SP, LANES), F32), pltpu.VMEM((2 * SP, LANES), F32),
                        pltpu.VMEM((2 * SP, LANES), F32), pltpu.VMEM((SP, LANES), F32),
                        pltpu.VMEM((2 * SP, LANES), F32), pltpu.VMEM((2 * SP, LANES), F32),
                        pltpu.VMEM((G, SP, LANES), F32), pltpu.VMEM((G, SP, LANES), F32),
                        pltpu.VMEM((G, SP, LANES), F32), pltpu.VMEM((SP, LANES), F32)],
        compiler_params=pltpu.CompilerParams(
            dimension_semantics=("parallel", "parallel", "parallel"),
            vmem_limit_bytes=VMEM_LIMIT),
        name="attn",
    )(aq, ak, ak, av, av, bias)


def _ffn_kernel(x_ref, yg_ref, ya_ref, mod_ref, wout_ref, g2_ref, wup_ref, cw_ref, cb_ref,
                wdn_ref, out_ref, carry_ref, act_ref):
    ts = FFN_SUB
    subs = [slice(i * ts, (i + 1) * ts) for i in range(FFN_TM // ts)]

    @pl.when(pl.program_id(1) == 0)
    def _():
        carry_ref[...] = jnp.zeros_like(carry_ref)

    rid = lax.broadcasted_iota(jnp.int32, (8, FFN_CH), 0)

    def conv(u, c0):
        cols = slice(c0, c0 + FFN_CH)
        c6 = carry_ref[6:7, cols]
        c7 = carry_ref[7:8, cols]
        r1 = pltpu.roll(u, 1, 0)
        r2 = pltpu.roll(u, 2, 0)
        u1 = jnp.concatenate([jnp.where(rid == 0, c7, r1[:8])] + [r1[8:]], axis=0)
        u2 = jnp.concatenate(
            [jnp.where(rid == 0, c6, jnp.where(rid == 1, c7, r2[:8]))] + [r2[8:]], axis=0)
        carry_ref[:, cols] = u[ts - 8:, :]
        return (cb_ref[:, cols] + cw_ref[0:1, cols] * u2 + cw_ref[1:2, cols] * u1
                + cw_ref[2:3, cols] * u)

    attn_out = [_dot(jnp.concatenate([yg_ref[rs, :], ya_ref[rs, :]], axis=-1), wout_ref[...])
                for rs in subs]
    hb = []
    for rs, a in zip(subs, attn_out):
        x1 = x_ref[rs, :] + mod_ref[2:3, :] * a
        out_ref[rs, :] = x1
        ms = jnp.mean(x1 * x1, axis=-1, keepdims=True)
        h = x1 * lax.rsqrt(ms + EPS) * g2_ref[...]
        hb.append((h * (1.0 + mod_ref[4:5, :]) + mod_ref[3:4, :]).astype(BF16))
    for rs, hs in zip(subs, hb):
        for c in range(D_FF // FFN_CH):
            g0 = c * FFN_CH
            v0 = D_FF + c * FFN_CH
            ug = conv(_dot(hs, wup_ref[:, g0:g0 + FFN_CH]), g0)
            uv = conv(_dot(hs, wup_ref[:, v0:v0 + FFN_CH]), v0)
            act_ref[rs, g0:g0 + FFN_CH] = ((ug / (1.0 + jnp.exp(-ug))) * uv).astype(BF16)
    for rs in subs:
        out_ref[rs, :] = out_ref[rs, :] + mod_ref[5:6, :] * _dot(act_ref[rs, :], wdn_ref[...])


def _ffn(x, yg, ya, mod3, w_out, g2, w_up, conv_w, conv_b, w_dn):
    B, S, _ = x.shape
    tm = FFN_TM
    tok = lambda n: pl.BlockSpec((None, tm, n), lambda b, i: (b, i, 0))
    return pl.pallas_call(
        _ffn_kernel,
        out_shape=jax.ShapeDtypeStruct((B, S, D_MODEL), F32),
        grid=(B, S // tm),
        in_specs=[tok(D_MODEL), tok(GLA_V), tok(ATTN_DIM),
                  pl.BlockSpec((None, N_MOD, D_MODEL), lambda b, i: (b, 0, 0)),
                  _const_spec(w_out.shape), _const_spec(g2.shape), _const_spec(w_up.shape),
                  _const_spec(conv_w.shape), _const_spec(conv_b.shape), _const_spec(w_dn.shape)],
        out_specs=tok(D_MODEL),
        scratch_shapes=[pltpu.VMEM((8, 2 * D_FF), F32), pltpu.VMEM((tm, D_FF), BF16)],
        compiler_params=pltpu.CompilerParams(
            dimension_semantics=("parallel", "arbitrary"), vmem_limit_bytes=VMEM_LIMIT),
        name="ffn",
    )(x, yg, ya, mod3, w_out, g2, w_up, conv_w, conv_b, w_dn)


def _head_ones():
    heads = MXU_DEPTH // ATTN_HEAD_DIM
    blk = np.kron(np.eye(heads, dtype=np.float32),
                  np.ones((ATTN_HEAD_DIM, ATTN_HEAD_DIM), np.float32))
    return jnp.asarray(blk, dtype=BF16)


def kernel(x, c, w_ada, b_ada, norm1_g, w_in, gla_w_gate, gla_b_gate, gla_norm_g, q_norm_g,
           k_norm_g, w_out, norm2_g, w_up, conv_w, conv_b, w_down):
    B = x.shape[0]
    depth = w_ada.shape[0]
    ones_blk = _head_ones()
    glr0 = 2 * GLA_QK + 2 * GLA_V
    for l in range(depth):
        mod3 = _ada(c, w_ada[l], b_ada[l]).reshape(B, N_MOD, D_MODEL)
        w = w_in[l]
        w_main = jnp.concatenate([w[:, :glr0], w[:, glr0 + GLA_GATE_RANK:]], axis=1).astype(BF16)
        w_glr = jnp.pad(w[:, glr0:glr0 + GLA_GATE_RANK],
                        ((0, 0), (0, LANES - GLA_GATE_RANK))).astype(BF16)
        w_gate = jnp.pad(gla_w_gate[l], ((0, LANES - GLA_GATE_RANK), (0, 0))).astype(BF16)
        gq, gk, la, gv, gr, aq, ak, av = _inproj(
            x, mod3, norm1_g[l].reshape(1, -1), w_main, w_glr, w_gate,
            gla_b_gate[l].reshape(1, -1), ones_blk,
            jnp.tile(q_norm_g[l], ATTN_HEADS).reshape(1, -1),
            jnp.tile(k_norm_g[l], ATTN_HEADS).reshape(1, -1))
        yg = _gla(gq, gk, la, gv, gr, gla_norm_g[l].reshape(1, -1))
        ya = _attn(aq, ak, av)
        x = _ffn(x, yg, ya, mod3, w_out[l].astype(BF16), norm2_g[l].reshape(1, -1),
                 w_up[l].astype(BF16), conv_w[l], conv_b[l].reshape(1, -1),
                 w_down[l].astype(BF16))
    return x
```

```python
import functools

import jax
import jax.numpy as jnp
import numpy as np
from jax import lax
from jax.experimental import pallas as pl
from jax.experimental.pallas import tpu as pltpu

F32 = jnp.float32
BF16 = jnp.bfloat16

D_MODEL = 1024
GLA_HEADS = 4
GLA_DK = 64
GLA_DV = 128
GLA_GATE_RANK = 16
GLA_GATE_TAU = 16.0
GLA_CHUNK = 64
GLA_QK = GLA_HEADS * GLA_DK
GLA_V = GLA_HEADS * GLA_DV
ATTN_HEADS = 8
ATTN_HEAD_DIM = 64
ATTN_DIM = ATTN_HEADS * ATTN_HEAD_DIM
DILATED_PAIRS = ((128, 1), (512, 4), (2048, 16))
ATTN_BLOCK = 128
D_FF = 2816
CONV_WIDTH = 3
N_MOD = 6
EPS = 1e-6

LANES = 128
MXU_DEPTH = 256
NEG_BIG = -1e30
LOG2E = 1.4426950408889634

INPROJ_TM = 1024
INPROJ_SUB = 512
GLA_T = 1024
GLA_SAFE_RANGE = 60.0
ATTN_SPAN = 2048
ATTN_FOLD = 4
ATTN_UNROLL = 6
ATTN_MERGE_ROWS = 64
FFN_TM = 1024
FFN_SUB = 256
FFN_CH = 256
VMEM_LIMIT = 56 * 1024 * 1024


def _const_spec(shape):
    nd = len(shape)
    return pl.BlockSpec(shape, lambda *_: (0,) * nd, pipeline_mode=pl.Buffered(1))


def _dot(a, b):
    return jnp.dot(a, b, preferred_element_type=F32)


def _dot_nt(a, b):
    return lax.dot_general(a, b, (((1,), (1,)), ((), ())), preferred_element_type=F32)


def _dot_tn(a, b):
    return lax.dot_general(a, b, (((0,), (0,)), ((), ())), preferred_element_type=F32)


def _split_bf16(x):
    hi = x.astype(BF16)
    lo = (x - hi.astype(F32)).astype(BF16)
    return hi, lo


def _ada_kernel(c_ref, w_ref, b_ref, o_ref):
    c = c_ref[...]
    cond = c / (1.0 + jnp.exp(-c))
    o_ref[...] = _dot(cond.astype(BF16), w_ref[...].astype(BF16)) + b_ref[...]


def _ada(c, w_ada, b_ada):
    B = c.shape[0]
    n = w_ada.shape[1]
    bn = D_MODEL
    return pl.pallas_call(
        _ada_kernel,
        out_shape=jax.ShapeDtypeStruct((B, n), F32),
        grid=(n // bn,),
        in_specs=[
            pl.BlockSpec((B, D_MODEL), lambda j: (0, 0)),
            pl.BlockSpec((D_MODEL, bn), lambda j: (0, j)),
            pl.BlockSpec((1, bn), lambda j: (0, j)),
        ],
        out_specs=pl.BlockSpec((B, bn), lambda j: (0, j)),
        name="ada",
    )(c, w_ada, b_ada.reshape(1, n))


def _inproj_kernel(x_ref, mod_ref, g1_ref, w_ref, wglr_ref, wgate_ref, bgate_ref, ones_ref,
                   qg_ref, kg_ref,
                   gq_ref, gk_ref, la_ref, gv_ref, gr_ref, aq_ref, ak_ref, av_ref):
    subs = [slice(i * INPROJ_SUB, (i + 1) * INPROJ_SUB) for i in range(INPROJ_TM // INPROJ_SUB)]

    def normed(rs):
        x = x_ref[rs, :]
        ms = jnp.mean(x * x, axis=-1, keepdims=True)
        y = x * lax.rsqrt(ms + EPS) * g1_ref[...]
        return (y * (1.0 + mod_ref[1:2, :]) + mod_ref[0:1, :]).astype(BF16)

    def head_norm(p, gain):
        gw = ones_ref.shape[0]
        hi, lo = _split_bf16(p * p)
        ssum = jnp.concatenate(
            [_dot(hi[:, c:c + gw], ones_ref[...]) + _dot(lo[:, c:c + gw], ones_ref[...])
             for c in range(0, ATTN_DIM, gw)], axis=-1)
        return p * lax.rsqrt(ssum * (1.0 / ATTN_HEAD_DIM) + EPS) * gain

    a0 = 2 * GLA_QK + 2 * GLA_V
    hbs = [normed(rs) for rs in subs]
    proj = lambda hb, c0, n: _dot(hb, w_ref[:, c0:c0 + n])
    early = [(_dot(hb, wglr_ref[...]), proj(hb, a0, ATTN_DIM), proj(hb, a0 + ATTN_DIM, ATTN_DIM))
             for hb in hbs]
    for rs, hb in zip(subs, hbs):
        gq_ref[rs, :] = (proj(hb, 0, GLA_QK) * (GLA_DK ** -0.5)).astype(gq_ref.dtype)
        gk_ref[rs, :] = proj(hb, GLA_QK, GLA_QK).astype(gk_ref.dtype)
        gv_ref[rs, :] = proj(hb, 2 * GLA_QK, GLA_V).astype(BF16)
        gr_ref[rs, :] = proj(hb, 2 * GLA_QK + GLA_V, GLA_V).astype(gr_ref.dtype)
        av_ref[rs, :] = proj(hb, a0 + 2 * ATTN_DIM, ATTN_DIM).astype(BF16)
    for rs, (glr, aq_raw, ak_raw) in zip(subs, early):
        z = _dot(glr.astype(BF16), wgate_ref[...]) + bgate_ref[...]
        log_sig = jnp.minimum(z, 0.0) - jnp.log1p(jnp.exp(-jnp.abs(z)))
        la_ref[rs, :] = log_sig * (1.0 / GLA_GATE_TAU)
        aq_ref[rs, :] = (head_norm(aq_raw, qg_ref[...])
                         * (ATTN_HEAD_DIM ** -0.5 * LOG2E)).astype(BF16)
        ak_ref[rs, :] = head_norm(ak_raw, kg_ref[...]).astype(BF16)


def _inproj(x, mod3, g1, w_main, w_glr, w_gate, b_gate, ones_blk, qg, kg):
    B, S, _ = x.shape
    tm = INPROJ_TM
    tok = lambda n: pl.BlockSpec((None, tm, n), lambda b, i: (b, i, 0))
    out_dims = [(GLA_QK, BF16), (GLA_QK, BF16), (GLA_QK, F32), (GLA_V, BF16), (GLA_V, BF16),
                (ATTN_DIM, BF16), (ATTN_DIM, BF16), (ATTN_DIM, BF16)]
    return pl.pallas_call(
        _inproj_kernel,
        out_shape=[jax.ShapeDtypeStruct((B, S, n), dt) for n, dt in out_dims],
        grid=(B, S // tm),
        in_specs=[
            tok(D_MODEL),
            pl.BlockSpec((None, N_MOD, D_MODEL), lambda b, i: (b, 0, 0)),
            _const_spec(g1.shape), _const_spec(w_main.shape), _const_spec(w_glr.shape),
            _const_spec(w_gate.shape), _const_spec(b_gate.shape), _const_spec(ones_blk.shape),
            _const_spec(qg.shape), _const_spec(kg.shape),
        ],
        out_specs=[tok(n) for n, _ in out_dims],
        compiler_params=pltpu.CompilerParams(
            dimension_semantics=("parallel", "parallel"), vmem_limit_bytes=VMEM_LIMIT),
        name="inproj",
    )(x, mod3, g1, w_main, w_glr, w_gate, b_gate, ones_blk, qg, kg)


def _pair_blockdiag(a, lane_lo):
    zero = jnp.zeros_like(a)
    return jnp.concatenate([jnp.where(lane_lo, a, zero), jnp.where(lane_lo, zero, a)],
                           axis=0).astype(BF16)


def _gla_kernel(q_ref, k_ref, la_ref, v_ref, r_ref, g_ref, tri_ref, o_ref,
                st_ref, st0_ref, raw_ref):
    C = GLA_CHUNK
    chunks = range(GLA_T // C)
    pairs = range(GLA_HEADS // 2)
    lanes = lambda p: slice(p * LANES, (p + 1) * LANES)
    rows = lambda c: slice(c * C, (c + 1) * C)
    vals = lambda h: slice(h * GLA_DV, (h + 1) * GLA_DV)

    @pl.when(pl.program_id(1) == 0)
    def _():
        st_ref[...] = jnp.zeros_like(st_ref)

    st0_ref[...] = st_ref[...]

    def finish(o, r, rsel, vs):
        ms = jnp.mean(o * o, axis=-1, keepdims=True)
        o = o * lax.rsqrt(ms + EPS) * g_ref[...]
        r = r.astype(F32)
        o_ref[rsel, vs] = (o * (r / (1.0 + jnp.exp(-r)))).astype(o_ref.dtype)

    row2 = lax.broadcasted_iota(jnp.int32, (2 * C, C), 0)
    col2 = lax.broadcasted_iota(jnp.int32, (2 * C, C), 1)
    causal = col2 <= jnp.where(row2 >= C, row2 - C, row2)
    lane_lo = lax.broadcasted_iota(jnp.int32, (C, LANES), 1) < GLA_DK
    lane_lo_v = lax.broadcasted_iota(jnp.int32, (GLA_DV, LANES), 1) < GLA_DK

    hi, lo = _split_bf16(la_ref[...])
    cum = [_dot(tri_ref[...], hi[rows(c), :]) + _dot(tri_ref[...], lo[rows(c), :])
           for c in chunks]

    q_state, q_in, k_in, k_out, decay = [], [], [], [], []
    for c in chunks:
        b = cum[c]
        b_last = b[C - 1:C, :]
        b_mid = b[C // 2:C // 2 + 1, :]
        q = q_ref[rows(c), :].astype(F32)
        k = k_ref[rows(c), :].astype(F32)
        qs = q * jnp.exp(b)
        qi = q * jnp.exp(b - b_mid)
        q_state.append([_pair_blockdiag(qs[:, lanes(p)], lane_lo) for p in pairs])
        q_in.append([_pair_blockdiag(qi[:, lanes(p)], lane_lo) for p in pairs])
        k_in.append((k * jnp.exp(b_mid - b)).astype(BF16))
        k_out.append((k * jnp.exp(b_last - b)).astype(BF16))
        decay.append(jnp.exp(b_last))

    scores, update = [], []
    for c in chunks:
        scores.append([_dot_nt(q_in[c][p], k_in[c][:, lanes(p)]) for p in pairs])
        update.append([[_dot_tn(v_ref[rows(c), vals(2 * p + hh)], k_out[c][:, lanes(p)])
                        for hh in range(2)] for p in pairs])
    scores = [[jnp.where(causal, s, 0.0).astype(BF16) for s in sc] for sc in scores]

    state = [st_ref[p] for p in pairs]
    seen = []
    for c in chunks:
        seen.append(state)
        state = [state[p] * decay[c][:, lanes(p)]
                 + jnp.where(lane_lo_v, update[c][p][0], update[c][p][1]) for p in pairs]
    for p in pairs:
        st_ref[p] = state[p]

    for c in chunks:
        for p in pairs:
            o_inter = _dot_nt(q_state[c][p], seen[c][p].astype(BF16))
            for hh in range(2):
                vs = vals(2 * p + hh)
                half = slice(hh * C, (hh + 1) * C)
                o = o_inter[half, :] + _dot(scores[c][p][half, :], v_ref[rows(c), vs])
                finish(o, r_ref[rows(c), vs], rows(c), vs)

    total = functools.reduce(jnp.minimum, [cum[c][C - 1:C, :] for c in chunks])
    risky = jnp.logical_not(jnp.min(total) >= -GLA_SAFE_RANGE)

    @pl.when(risky)
    def _():
        G = 16
        sub = lax.broadcasted_iota(jnp.int32, (G, LANES), 0)
        lane_lo_t = lax.broadcasted_iota(jnp.int32, (G, LANES), 1) < GLA_DK
        raw_ref[...] = jnp.zeros_like(raw_ref)

        def token(t, st):
            grp = pl.ds(pl.multiple_of(lax.shift_left(lax.shift_right_logical(t, 4), 4), G), G)
            here = sub == jnp.bitwise_and(t, G - 1)
            pick = lambda x: jnp.where(here, x, jnp.zeros_like(x))
            new = []
            for p in pairs:
                a = jnp.exp(jnp.sum(pick(la_ref[grp, lanes(p)]), axis=0, keepdims=True))
                k16 = pick(k_ref[grp, lanes(p)]).astype(BF16)
                upd = [_dot_tn(pick(v_ref[grp, vals(2 * p + hh)]), k16)
                       for hh in range(2)]
                sp = st[p] * a + jnp.where(lane_lo_v, upd[0], upd[1])
                o = _dot_nt(_pair_blockdiag(pick(q_ref[grp, lanes(p)]), lane_lo_t),
                            sp.astype(BF16))
                raw_ref[grp, vals(2 * p)] += o[:G, :]
                raw_ref[grp, vals(2 * p + 1)] += o[G:, :]
                new.append(sp)
            return tuple(new)

        st = lax.fori_loop(0, GLA_T, token, tuple(st0_ref[p] for p in pairs))
        for p in pairs:
            st_ref[p] = st[p]
        for c in chunks:
            for h in range(GLA_HEADS):
                finish(raw_ref[rows(c), vals(h)], r_ref[rows(c), vals(h)], rows(c), vals(h))


def _chunk_prefix_matrix():
    i = np.arange(GLA_CHUNK)
    return jnp.asarray(i[None, :] <= i[:, None], dtype=BF16)


def _gla(gq, gk, la, gv, gr, gain):
    B, S, _ = gq.shape
    T = GLA_T
    tri = _chunk_prefix_matrix()
    tok = lambda n: pl.BlockSpec((None, T, n), lambda b, j: (b, j, 0))
    return pl.pallas_call(
        _gla_kernel,
        out_shape=jax.ShapeDtypeStruct((B, S, GLA_V), BF16),
        grid=(B, S // T),
        in_specs=[tok(GLA_QK), tok(GLA_QK), tok(GLA_QK), tok(GLA_V), tok(GLA_V),
                  _const_spec(gain.shape), _const_spec(tri.shape)],
        out_specs=tok(GLA_V),
        scratch_shapes=[pltpu.VMEM((GLA_HEADS // 2, GLA_DV, LANES), F32),
                        pltpu.VMEM((GLA_HEADS // 2, GLA_DV, LANES), F32),
                        pltpu.VMEM((T, GLA_V), F32)],
        compiler_params=pltpu.CompilerParams(
            dimension_semantics=("parallel", "arbitrary"), vmem_limit_bytes=VMEM_LIMIT),
        name="gla",
    )(gq, gk, la, gv, gr, gain, tri)


def _attn_kernel(q_ref, k_ref, kp_ref, v_ref, vp_ref, bias_ref, y_ref,
                 qf, kf, vf, q4, k4, v4, og, mg, dg, ys):
    SP, BL, F = ATTN_SPAN, ATTN_BLOCK, ATTN_FOLD
    QR, KR = SP // F, 2 * SP // F
    first_span = pl.program_id(2) == 0
    qf[...] = q_ref[...].astype(F32)
    kf[0:SP, :] = kp_ref[...].astype(F32)
    kf[SP:, :] = k_ref[...].astype(F32)
    vf[0:SP, :] = vp_ref[...].astype(F32)
    vf[SP:, :] = v_ref[...].astype(F32)
    for r in range(F):
        q4[r * QR:(r + 1) * QR, :] = qf[pl.ds(r, QR, stride=F), :]
        k4[r * KR:(r + 1) * KR, :] = kf[pl.ds(r, KR, stride=F), :]
        v4[r * KR:(r + 1) * KR, :] = vf[pl.ds(r, KR, stride=F), :]
    lane_lo = lax.broadcasted_iota(jnp.int32, (BL, LANES), 1) < ATTN_HEAD_DIM
    ones = jnp.ones((2 * BL, LANES), BF16)

    (_, d0), (_, d1), (_, d2) = DILATED_PAIRS
    assert d0 == 1 and d1 == F and d2 % F == 0 and SP == BL * d2
    s2 = d2 // F
    load = lambda buf, rows: (lambda: buf[rows, :].astype(BF16))
    across = lambda prev, cur: (lambda: jnp.concatenate([prev[SP - BL:SP, :], cur[0:BL, :]], axis=0))
    per_dilation = [[], [], []]
    for n in range(SP // BL):
        keys = pl.ds((n - 1) * BL, 2 * BL)
        per_dilation[0].append((0, load(q_ref, pl.ds(n * BL, BL)),
                                load(k_ref, keys) if n else across(kp_ref, k_ref),
                                load(v_ref, keys) if n else across(vp_ref, v_ref),
                                pl.ds(n * BL, BL), n == 0))
    for r in range(F):
        for n in range(QR // BL):
            keys = pl.ds(r * KR + QR + (n - 1) * BL, 2 * BL)
            per_dilation[1].append((1, load(q4, pl.ds(r * QR + n * BL, BL)), load(k4, keys),
                                    load(v4, keys), pl.ds(r * QR + n * BL, BL), n == 0))
    for r in range(F):
        for a in range(s2):
            keys = pl.ds(r * KR + a, 2 * BL, stride=s2)
            rows = pl.ds(r * QR + a, BL, stride=s2)
            per_dilation[2].append((2, load(q4, rows), load(k4, keys), load(v4, keys), rows, True))
    tiles = [t for same_index in zip(*per_dilation) for t in same_index]
    groups = [tiles[i:i + ATTN_UNROLL] for i in range(0, len(tiles), ATTN_UNROLL)]

    def score_stage(group):
        return [_dot_nt(_pair_blockdiag(q(), lane_lo), k()) for _, q, k, _, _, _ in group]

    def softmax_stage(group, scores):
        out = []
        for (g, *_, seq_start), s in zip(group, scores):
            sel = jnp.where(first_span, 0, 1) if seq_start else 1
            s = s + bias_ref[g, sel]
            m = jnp.max(s, axis=-1, keepdims=True)
            out.append((m, jnp.exp2(s - m).astype(BF16)))
        return out

    def value_stage(group, stats):
        accs = [_dot(e, jnp.concatenate([v(), ones], axis=1))
                for (_, _, _, v, _, _), (_, e) in zip(group, stats)]
        for (g, _, _, _, rows, _), (m, _), acc in zip(group, stats, accs):
            og[g, rows, :] = jnp.where(lane_lo, acc[:BL, :LANES], acc[BL:, :LANES])
            mg[g, rows, :] = jnp.where(lane_lo, m[:BL], m[BL:])
            dg[g, rows, :] = jnp.where(lane_lo, acc[:BL, LANES:], acc[BL:, LANES:])

    scores = score_stage(groups[0])
    for i, group in enumerate(groups):
        ahead = score_stage(groups[i + 1]) if i + 1 < len(groups) else None
        value_stage(group, softmax_stage(group, scores))
        scores = ahead

    R = ATTN_MERGE_ROWS

    def merge(c, carry):
        res = pl.ds(pl.multiple_of(c * R, R), R)
        r = c // (QR // R)
        j0 = (c - r * (QR // R)) * R
        tok = pl.ds(r + F * j0, R, stride=F)
        rows = (tok, res, res)
        ms = [mg[g, rows[g], :] for g in range(3)]
        top = functools.reduce(jnp.maximum, ms)
        ws = [jnp.exp2(mi - top) for mi in ms]
        num = sum(w * og[g, rows[g], :] for g, w in enumerate(ws))
        den = sum(w * dg[g, rows[g], :] for g, w in enumerate(ws))
        ys[tok, :] = num / den
        return carry

    lax.fori_loop(0, SP // R, merge, 0, unroll=4)
    y_ref[...] = ys[...].astype(y_ref.dtype)


def _attn_bias():
    span = DILATED_PAIRS[0][0] // DILATED_PAIRS[0][1]
    slopes = np.asarray([2.0 ** (-8.0 * (h + 1) / ATTN_HEADS) for h in range(ATTN_HEADS)],
                        dtype=np.float32)
    iq = np.arange(ATTN_BLOCK)[:, None]
    ik = np.arange(2 * ATTN_BLOCK)[None, :]
    rel = iq + ATTN_BLOCK - ik
    valid = (rel >= 0) & (rel <= span)
    out = np.empty((len(DILATED_PAIRS), 2, ATTN_HEADS, ATTN_BLOCK, 2 * ATTN_BLOCK), np.float32)
    for g, (_, d) in enumerate(DILATED_PAIRS):
        for first in (0, 1):
            ok = valid & ((ik >= ATTN_BLOCK) if first == 0 else True)
            for h in range(ATTN_HEADS):
                alibi = -slopes[h] * (d * rel).astype(np.float32) * np.float32(LOG2E)
                out[g, first, h] = np.where(ok, alibi, np.float32(NEG_BIG))
    return out.reshape(len(DILATED_PAIRS), 2, ATTN_HEADS // 2, 2 * ATTN_BLOCK, 2 * ATTN_BLOCK)


def _attn(aq, ak, av):
    B, S, _ = aq.shape
    SP = ATTN_SPAN
    G = len(DILATED_PAIRS)
    bias = jnp.asarray(_attn_bias())
    cur = pl.BlockSpec((None, SP, LANES), lambda p, b, j: (b, j, p))
    prev = pl.BlockSpec((None, SP, LANES), lambda p, b, j: (b, jnp.maximum(j - 1, 0), p))
    return pl.pallas_call(
        _attn_kernel,
        out_shape=jax.ShapeDtypeStruct((B, S, ATTN_DIM), BF16),
        grid=(ATTN_HEADS // 2, B, S // SP),
        in_specs=[cur, cur, prev, cur, prev,
                  pl.BlockSpec((G, 2, None, 2 * ATTN_BLOCK, 2 * ATTN_BLOCK),
                               lambda p, b, j: (0, 0, p, 0, 0))],
        out_specs=cur,
        scratch_shapes=[pltpu.VMEM((SP, LANES), F32), pltpu.VMEM((2 * SP, LANES), F32),
                        pltpu.VMEM((2 * SP, LANES), F32), pltpu.VMEM((SP, LANES), F32),
                        pltpu.VMEM((2 * SP, LANES), F32), pltpu.VMEM((2 * SP, LANES), F32),
                        pltpu.VMEM((G, SP, LANES), F32), pltpu.VMEM((G, SP, LANES), F32),
                        pltpu.VMEM((G, SP, LANES), F32), pltpu.VMEM((SP, LANES), F32)],
        compiler_params=pltpu.CompilerParams(
            dimension_semantics=("parallel", "parallel", "parallel"),
            vmem_limit_bytes=VMEM_LIMIT),
        name="attn",
    )(aq, ak, ak, av, av, bias)


def _ffn_kernel(x_ref, yg_ref, ya_ref, mod_ref, wout_ref, g2_ref, wup_ref, cw_ref, cb_ref,
                wdn_ref, out_ref, carry_ref, act_ref):
    ts = FFN_SUB
    subs = [slice(i * ts, (i + 1) * ts) for i in range(FFN_TM // ts)]

    @pl.when(pl.program_id(1) == 0)
    def _():
        carry_ref[...] = jnp.zeros_like(carry_ref)

    rid = lax.broadcasted_iota(jnp.int32, (8, FFN_CH), 0)

    def conv(u, c0):
        cols = slice(c0, c0 + FFN_CH)
        c6 = carry_ref[6:7, cols]
        c7 = carry_ref[7:8, cols]
        r1 = pltpu.roll(u, 1, 0)
        r2 = pltpu.roll(u, 2, 0)
        u1 = jnp.concatenate([jnp.where(rid == 0, c7, r1[:8])] + [r1[8:]], axis=0)
        u2 = jnp.concatenate(
            [jnp.where(rid == 0, c6, jnp.where(rid == 1, c7, r2[:8]))] + [r2[8:]], axis=0)
        carry_ref[:, cols] = u[ts - 8:, :]
        return (cb_ref[:, cols] + cw_ref[0:1, cols] * u2 + cw_ref[1:2, cols] * u1
                + cw_ref[2:3, cols] * u)

    attn_out = [_dot(jnp.concatenate([yg_ref[rs, :], ya_ref[rs, :]], axis=-1), wout_ref[...])
                for rs in subs]
    hb = []
    for rs, a in zip(subs, attn_out):
        x1 = x_ref[rs, :] + mod_ref[2:3, :] * a
        out_ref[rs, :] = x1
        ms = jnp.mean(x1 * x1, axis=-1, keepdims=True)
        h = x1 * lax.rsqrt(ms + EPS) * g2_ref[...]
        hb.append((h * (1.0 + mod_ref[4:5, :]) + mod_ref[3:4, :]).astype(BF16))
    for rs, hs in zip(subs, hb):
        for c in range(D_FF // FFN_CH):
            g0 = c * FFN_CH
            v0 = D_FF + c * FFN_CH
            ug = conv(_dot(hs, wup_ref[:, g0:g0 + FFN_CH]), g0)
            uv = conv(_dot(hs, wup_ref[:, v0:v0 + FFN_CH]), v0)
            act_ref[rs, g0:g0 + FFN_CH] = ((ug / (1.0 + jnp.exp(-ug))) * uv).astype(BF16)
    for rs in subs:
        out_ref[rs, :] = out_ref[rs, :] + mod_ref[5:6, :] * _dot(act_ref[rs, :], wdn_ref[...])


def _ffn(x, yg, ya, mod3, w_out, g2, w_up, conv_w, conv_b, w_dn):
    B, S, _ = x.shape
    tm = FFN_TM
    tok = lambda n: pl.BlockSpec((None, tm, n), lambda b, i: (b, i, 0))
    return pl.pallas_call(
        _ffn_kernel,
        out_shape=jax.ShapeDtypeStruct((B, S, D_MODEL), F32),
        grid=(B, S // tm),
        in_specs=[tok(D_MODEL), tok(GLA_V), tok(ATTN_DIM),
                  pl.BlockSpec((None, N_MOD, D_MODEL), lambda b, i: (b, 0, 0)),
                  _const_spec(w_out.shape), _const_spec(g2.shape), _const_spec(w_up.shape),
                  _const_spec(conv_w.shape), _const_spec(conv_b.shape), _const_spec(w_dn.shape)],
        out_specs=tok(D_MODEL),
        scratch_shapes=[pltpu.VMEM((8, 2 * D_FF), F32), pltpu.VMEM((tm, D_FF), BF16)],
        compiler_params=pltpu.CompilerParams(
            dimension_semantics=("parallel", "arbitrary"), vmem_limit_bytes=VMEM_LIMIT),
        name="ffn",
    )(x, yg, ya, mod3, w_out, g2, w_up, conv_w, conv_b, w_dn)


def _head_ones():
    heads = MXU_DEPTH // ATTN_HEAD_DIM
    blk = np.kron(np.eye(heads, dtype=np.float32),
                  np.ones((ATTN_HEAD_DIM, ATTN_HEAD_DIM), np.float32))
    return jnp.asarray(blk, dtype=BF16)


def kernel(x, c, w_ada, b_ada, norm1_g, w_in, gla_w_gate, gla_b_gate, gla_norm_g, q_norm_g,
           k_norm_g, w_out, norm2_g, w_up, conv_w, conv_b, w_down):
    B = x.shape[0]
    depth = w_ada.shape[0]
    ones_blk = _head_ones()
    glr0 = 2 * GLA_QK + 2 * GLA_V
    for l in range(depth):
        mod3 = _ada(c, w_ada[l], b_ada[l]).reshape(B, N_MOD, D_MODEL)
        w = w_in[l]
        w_main = jnp.concatenate([w[:, :glr0], w[:, glr0 + GLA_GATE_RANK:]], axis=1).astype(BF16)
        w_glr = jnp.pad(w[:, glr0:glr0 + GLA_GATE_RANK],
                        ((0, 0), (0, LANES - GLA_GATE_RANK))).astype(BF16)
        w_gate = jnp.pad(gla_w_gate[l], ((0, LANES - GLA_GATE_RANK), (0, 0))).astype(BF16)
        gq, gk, la, gv, gr, aq, ak, av = _inproj(
            x, mod3, norm1_g[l].reshape(1, -1), w_main, w_glr, w_gate,
            gla_b_gate[l].reshape(1, -1), ones_blk,
            jnp.tile(q_norm_g[l], ATTN_HEADS).reshape(1, -1),
            jnp.tile(k_norm_g[l], ATTN_HEADS).reshape(1, -1))
        yg = _gla(gq, gk, la, gv, gr, gla_norm_g[l].reshape(1, -1))
        ya = _attn(aq, ak, av)
        x = _ffn(x, yg, ya, mod3, w_out[l].astype(BF16), norm2_g[l].reshape(1, -1),
                 w_up[l].astype(BF16), conv_w[l], conv_b[l].reshape(1, -1),
                 w_down[l].astype(BF16))
    return x
```
